```python
import jax, jax.numpy as jnp
from jax import lax
import numpy as np

D_MODEL = 1024
BATCH = 1
SEQ = 16384
DEPTH = 2

CHUNK = 64
N_EVEN = (DEPTH + 1) // 2
N_ODD = DEPTH // 2
POOL_WINDOWS = (2, 4, 8, 16)
N_POOL_GROUPS = 4
POOL_WIDTH = D_MODEL // 2
POOL_GROUP = POOL_WIDTH // N_POOL_GROUPS
SB_HEAD_DIM = 64
SB_HEADS = (D_MODEL // 2) // SB_HEAD_DIM
SB_WIDTH = SB_HEADS * SB_HEAD_DIM
SB_BLOCK = 128
IN_WIDTH = POOL_WIDTH + 3 * SB_WIDTH
MIX_WIDTH = POOL_WIDTH + SB_WIDTH
RW_HEAD = 64
RW_HEADS = D_MODEL // RW_HEAD
W_LORA = 64
A_LORA = 64
G_LORA = 160
GN_EPS = 64e-5
N_EXPERTS = 64
TOP_K = 8
N_GROUPS = 8
TOPK_GROUPS = 4
EXPERT_FF = 256
SHARED_FF = 256
ROUTED_SCALE = 2.5
MOE_BLOCK = 128
DN_ALPHA = (2.0 * DEPTH) ** 0.25
DN_BETA = (8.0 * DEPTH) ** -0.25
LN_EPS = 1e-5

kernel_name = 'hybrid_pool_stickbreak_rwkv7_moe'


def _layer_norm(x, g, b):
    xf = x.astype(jnp.float32)
    mu = jnp.mean(xf, axis=-1, keepdims=True)
    var = jnp.mean(jnp.square(xf - mu), axis=-1, keepdims=True)
    return ((xf - mu) * lax.rsqrt(var + LN_EPS) * g + b).astype(x.dtype)


def _pool_mixer(u, pool_w, pool_scale):
    b, s, _ = u.shape
    uf = u.astype(jnp.float32).reshape(b, s, N_POOL_GROUPS, POOL_GROUP)
    csum = jnp.concatenate([jnp.zeros((b, 1, N_POOL_GROUPS, POOL_GROUP), jnp.float32),
                            jnp.cumsum(uf, axis=1)], axis=1)
    t = jnp.arange(s)
    means = []
    for g, win in enumerate(POOL_WINDOWS):
        lo = jnp.maximum(t + 1 - win, 0)
        cnt = (t + 1 - lo).astype(jnp.float32)
        window_sum = csum[:, 1:, g] - jnp.take(csum[:, :, g], lo, axis=1)
        means.append(window_sum / cnt[None, :, None])
    pooled = jnp.stack(means, axis=2)
    diff = (pooled - uf).astype(u.dtype)
    y = jnp.einsum('bsgc,gcd->bsgd', diff, pool_w)
    return y.reshape(b, s, POOL_WIDTH) * pool_scale


def _stick_breaking(q, k, v):
    b, h, s, dh = q.shape
    nb = s // SB_BLOCK
    qb = q.astype(jnp.float32).reshape(b, h, nb, SB_BLOCK, dh).transpose(2, 0, 1, 3, 4)
    kf = k.astype(jnp.float32)
    vf = v.astype(jnp.float32)
    key_pos = jnp.arange(s)
    scale = dh ** -0.5

    def block(args):
        qi, bi = args
        z = jnp.einsum('bhqd,bhkd->bhqk', qi, kf) * scale
        q_pos = bi * SB_BLOCK + jnp.arange(SB_BLOCK)
        past = key_pos[None, :] < q_pos[:, None]
        log_fail = jnp.where(past, -jax.nn.softplus(z), 0.0)
        between = lax.cumsum(log_fail, axis=3, reverse=True) - log_fail
        w = jnp.where(past, jnp.exp(jax.nn.log_sigmoid(z) + between), 0.0)
        return jnp.einsum('bhqk,bhkd->bhqd', w, vf)

    o = lax.map(block, (qb, jnp.arange(nb)))
    return o.transpose(1, 2, 0, 3, 4).reshape(b, h, s, dh).astype(q.dtype)


def _even_mixer(x, w_in, pool_w, pool_scale, w_out):
    b, s, _ = x.shape
    hcat = x @ w_in
    u, q, k, v = jnp.split(hcat, [POOL_WIDTH, POOL_WIDTH + SB_WIDTH, POOL_WIDTH + 2 * SB_WIDTH], axis=-1)
    y_pool = _pool_mixer(u, pool_w, pool_scale)
    to_heads = lambda t: t.reshape(b, s, SB_HEADS, SB_HEAD_DIM).transpose(0, 2, 1, 3)
    y_sb = _stick_breaking(to_heads(q), to_heads(k), to_heads(v))
    y_sb = y_sb.transpose(0, 2, 1, 3).reshape(b, s, SB_WIDTH)
    return jnp.concatenate([y_pool, y_sb], axis=-1) @ w_out


def _rwkv7_mixer(x, mu, w_r, w_k, w_v, w_w1, w_w2, w0, a1, a2, a0, g1, g2,
                 k_k, k_a, r_k, lnx_g, lnx_b, w_o):
    b, s, d = x.shape
    hh, n = RW_HEADS, RW_HEAD
    xx = jnp.pad(x, ((0, 0), (1, 0), (0, 0)))[:, :-1] - x
    xr, xw, xk, xv, xa, xg = [x + xx * mu[i] for i in range(6)]
    r = xr @ w_r
    w_log = -jax.nn.softplus(-(w0 + jnp.tanh(xw @ w_w1) @ w_w2)) - 0.5
    k = xk @ w_k
    v = xv @ w_v
    a = jax.nn.sigmoid(a0 + (xa @ a1) @ a2)
    g = jax.nn.sigmoid(xg @ g1) @ g2
    heads = lambda t: t.astype(jnp.float32).reshape(b, s, hh, n)
    kk = heads(k * k_k)
    kk = kk / jnp.maximum(jnp.sqrt(jnp.sum(jnp.square(kk), axis=-1, keepdims=True)), 1e-12)
    k = k * (1.0 + (a - 1.0) * k_a)
    rh, kh, vh, ah = heads(r), heads(k), heads(v), heads(a)
    decay = jnp.exp(-jnp.exp(heads(w_log)))

    def step(state, inp):
        r_t, d_t, k_t, v_t, kk_t, a_t = inp
        sa = jnp.einsum('bhvk,bhk->bhv', state, kk_t)
        state = (state * d_t[:, :, None, :] - sa[..., None] * (kk_t * a_t)[:, :, None, :]
                 + v_t[..., None] * k_t[:, :, None, :])
        return state, jnp.einsum('bhvk,bhk->bhv', state, r_t)

    seq_major = lambda t: jnp.swapaxes(t, 0, 1)
    inputs = (seq_major(rh), seq_major(decay), seq_major(kh), seq_major(vh), seq_major(kk), seq_major(ah))
    _, y = lax.scan(step, jnp.zeros((b, hh, n, n), jnp.float32), inputs)
    y = jnp.swapaxes(y, 0, 1)
    ym = jnp.mean(y, axis=-1, keepdims=True)
    yv = jnp.mean(jnp.square(y - ym), axis=-1, keepdims=True)
    y = ((y - ym) * lax.rsqrt(yv + GN_EPS)).reshape(b, s, d) * lnx_g + lnx_b
    bonus = jnp.sum(rh * kh * r_k, axis=-1, keepdims=True) * vh
    y = (y + bonus.reshape(b, s, d)).astype(x.dtype)
    return (y * g) @ w_o


def _swiglu(h, wg, wu, wd):
    return (jax.nn.silu(h @ wg) * (h @ wu)) @ wd


def _routed_experts(xf, idx, gates, w_gate, w_up, w_down):
    n, d = xf.shape
    n_assign = n * TOP_K
    n_blocks = -(-n_assign // MOE_BLOCK) + N_EXPERTS
    n_slots = n_blocks * MOE_BLOCK
    e_flat = idx.reshape(-1)
    tok_flat = jnp.repeat(jnp.arange(n, dtype=jnp.int32), TOP_K)
    g_flat = gates.reshape(-1)
    order = jnp.argsort(e_flat)
    e_sorted = e_flat[order]
    counts = jnp.zeros((N_EXPERTS,), jnp.int32).at[e_flat].add(1)
    padded = (counts + MOE_BLOCK - 1) // MOE_BLOCK * MOE_BLOCK
    start = jnp.cumsum(counts) - counts
    pad_end = jnp.cumsum(padded)
    pad_start = pad_end - padded
    dest = pad_start[e_sorted] + (jnp.arange(n_assign, dtype=jnp.int32) - start[e_sorted])
    slot_tok = jnp.full((n_slots,), n, jnp.int32).at[dest].set(tok_flat[order])
    slot_gate = jnp.zeros((n_slots,), jnp.float32).at[dest].set(g_flat[order])
    block_start = jnp.arange(n_blocks, dtype=jnp.int32) * MOE_BLOCK
    block_expert = jnp.minimum(jnp.searchsorted(pad_end, block_start, side='right'), N_EXPERTS - 1)
    x_pad = jnp.concatenate([xf, jnp.zeros((1, d), xf.dtype)], axis=0)
    xs = x_pad[slot_tok].reshape(n_blocks, MOE_BLOCK, d)

    def expert_block(args):
        xb, e = args
        return _swiglu(xb, w_gate[e], w_up[e], w_down[e])

    ys = lax.map(expert_block, (xs, block_expert)).reshape(n_slots, d)
    ys = ys.astype(jnp.float32) * slot_gate[:, None]
    out = jnp.zeros((n + 1, d), jnp.float32).at[slot_tok].add(ys)[:n]
    return out.astype(xf.dtype)


def _moe(x, w_router, e_bias, w_gate, w_up, w_down, ws_gate, ws_up, ws_down):
    b, s, d = x.shape
    xf = x.reshape(b * s, d)
    n = xf.shape[0]
    scores = jax.nn.sigmoid((xf @ w_router).astype(jnp.float32))
    biased = scores + e_bias
    grp_score = jnp.sum(lax.top_k(biased.reshape(n, N_GROUPS, N_EXPERTS // N_GROUPS), 2)[0], axis=-1)
    _, top_grp = lax.top_k(grp_score, TOPK_GROUPS)
    grp_mask = jnp.any(top_grp[:, :, None] == jnp.arange(N_GROUPS)[None, None, :], axis=1)
    expert_mask = jnp.repeat(grp_mask, N_EXPERTS // N_GROUPS, axis=1)
    _, idx = lax.top_k(jnp.where(expert_mask, biased, -jnp.inf), TOP_K)
    sel = jnp.take_along_axis(scores, idx, axis=1)
    gates = sel / jnp.sum(sel, axis=-1, keepdims=True) * ROUTED_SCALE
    routed = _routed_experts(xf, idx, gates, w_gate, w_up, w_down)
    shared = _swiglu(xf, ws_gate, ws_up, ws_down)
    return (routed + shared).reshape(b, s, d)


def setup_inputs(seed: int = 0) -> dict:
    key = jax.random.key(seed)
    ks = iter(jax.random.split(key, 48))
    f32 = jnp.float32
    nrm = lambda shape, sc: sc * jax.random.normal(next(ks), shape, f32)
    uni = lambda shape, lo, hi: jax.random.uniform(next(ks), shape, f32, lo, hi)
    D = D_MODEL
    return {
        'x': nrm((BATCH, SEQ, D), 1.0),
        'ln_mix_g': 1.0 + nrm((DEPTH, D), 0.05),
        'ln_mix_b': nrm((DEPTH, D), 0.01),
        'ln_ffn_g': 1.0 + nrm((DEPTH, D), 0.05),
        'ln_ffn_b': nrm((DEPTH, D), 0.01),
        'ab_w_in': nrm((N_EVEN, D, IN_WIDTH), D ** -0.5),
        'ab_pool_w': nrm((N_EVEN, N_POOL_GROUPS, POOL_GROUP, POOL_GROUP), POOL_GROUP ** -0.5),
        'ab_pool_scale': 1.0 + nrm((N_EVEN, POOL_WIDTH), 0.1),
        'ab_w_out': nrm((N_EVEN, MIX_WIDTH, D), DN_BETA * MIX_WIDTH ** -0.5),
        'rw_mu': uni((N_ODD, 6, D), 0.0, 1.0),
        'rw_w_r': nrm((N_ODD, D, D), D ** -0.5),
        'rw_w_k': nrm((N_ODD, D, D), D ** -0.5),
        'rw_w_v': nrm((N_ODD, D, D), D ** -0.5),
        'rw_w_w1': nrm((N_ODD, D, W_LORA), D ** -0.5),
        'rw_w_w2': nrm((N_ODD, W_LORA, D), 0.5 * W_LORA ** -0.5),
        'rw_w0': uni((N_ODD, D), -6.0, -1.0),
        'rw_a1': nrm((N_ODD, D, A_LORA), D ** -0.5),
        'rw_a2': nrm((N_ODD, A_LORA, D), 0.5 * A_LORA ** -0.5),
        'rw_a0': nrm((N_ODD, D), 0.5),
        'rw_g1': nrm((N_ODD, D, G_LORA), D ** -0.5),
        'rw_g2': nrm((N_ODD, G_LORA, D), G_LORA ** -0.5),
        'rw_k_k': 0.85 + nrm((N_ODD, D), 0.05),
        'rw_k_a': 1.0 + nrm((N_ODD, D), 0.05),
        'rw_r_k': nrm((N_ODD, RW_HEADS, RW_HEAD), 0.1),
        'rw_lnx_g': 1.0 + nrm((N_ODD, D), 0.05),
        'rw_lnx_b': nrm((N_ODD, D), 0.01),
        'rw_w_o': nrm((N_ODD, D, D), DN_BETA * D ** -0.5),
        'moe_w_router': nrm((DEPTH, D, N_EXPERTS), D ** -0.5),
        'moe_e_bias': nrm((DEPTH, N_EXPERTS), 0.01),
        'moe_w_gate': nrm((DEPTH, N_EXPERTS, D, EXPERT_FF), D ** -0.5),
        'moe_w_up': nrm((DEPTH, N_EXPERTS, D, EXPERT_FF), D ** -0.5),
        'moe_w_down': nrm((DEPTH, N_EXPERTS, EXPERT_FF, D), DN_BETA * EXPERT_FF ** -0.5),
        'moe_ws_gate': nrm((DEPTH, D, SHARED_FF), D ** -0.5),
        'moe_ws_up': nrm((DEPTH, D, SHARED_FF), D ** -0.5),
        'moe_ws_down': nrm((DEPTH, SHARED_FF, D), DN_BETA * SHARED_FF ** -0.5),
    }


def reference(x, ln_mix_g, ln_mix_b, ln_ffn_g, ln_ffn_b,
              ab_w_in, ab_pool_w, ab_pool_scale, ab_w_out,
              rw_mu, rw_w_r, rw_w_k, rw_w_v, rw_w_w1, rw_w_w2, rw_w0, rw_a1, rw_a2, rw_a0,
              rw_g1, rw_g2, rw_k_k, rw_k_a, rw_r_k, rw_lnx_g, rw_lnx_b, rw_w_o,
              moe_w_router, moe_e_bias, moe_w_gate, moe_w_up, moe_w_down,
              moe_ws_gate, moe_ws_up, moe_ws_down):
    for layer in range(DEPTH):
        i = layer // 2
        if layer % 2 == 0:
            mix = _even_mixer(x, ab_w_in[i], ab_pool_w[i], ab_pool_scale[i], ab_w_out[i])
        else:
            mix = _rwkv7_mixer(x, rw_mu[i], rw_w_r[i], rw_w_k[i], rw_w_v[i], rw_w_w1[i], rw_w_w2[i],
                               rw_w0[i], rw_a1[i], rw_a2[i], rw_a0[i], rw_g1[i], rw_g2[i],
                               rw_k_k[i], rw_k_a[i], rw_r_k[i], rw_lnx_g[i], rw_lnx_b[i], rw_w_o[i])
        x = _layer_norm(DN_ALPHA * x + mix, ln_mix_g[layer], ln_mix_b[layer])
        ff = _moe(x, moe_w_router[layer], moe_e_bias[layer], moe_w_gate[layer], moe_w_up[layer],
                  moe_w_down[layer], moe_ws_gate[layer], moe_ws_up[layer], moe_ws_down[layer])
        x = _layer_norm(DN_ALPHA * x + ff, ln_ffn_g[layer], ln_ffn_b[layer])
    return x
```

```python
import functools

import jax
import jax.numpy as jnp
from jax import lax
from jax.experimental import pallas as pl
from jax.experimental.pallas import tpu as pltpu

F32 = jnp.float32
BF16 = jnp.bfloat16

D_MODEL = 1024
DEPTH = 2
POOL_WINDOWS = (2, 4, 8, 16)
POOL_WIDTH = 512
POOL_GROUP = 128
POOL_HALO = 16
SB_HEAD_DIM = 64
SB_WIDTH = 512
RW_HEAD = 64
GN_EPS = 64e-5
N_EXPERTS = 64
TOP_K = 8
N_GROUPS = 8
GROUP_SIZE = N_EXPERTS // N_GROUPS
TOPK_GROUPS = 4
EXPERT_FF = 256
ROUTED_SCALE = 2.5
DN_ALPHA = (2.0 * DEPTH) ** 0.25
LN_EPS = 1e-5

LANES = 128
SUBLANES = 8
EXP_ZERO_BOUND = 110.0

_HIGHEST = lax.Precision.HIGHEST


def _dot(a, b, **kw):
    return jnp.dot(a, b, preferred_element_type=F32, **kw)


def _dot_nt(a, b, **kw):
    return lax.dot_general(a, b, (((1,), (1,)), ((), ())), preferred_element_type=F32, **kw)


def _split_bf16(x):
    hi = x.astype(BF16)
    lo = (x - hi.astype(F32)).astype(BF16)
    return hi, lo


def _layer_norm(h, g, b):
    mu = jnp.mean(h, axis=-1, keepdims=True)
    c = h - mu
    var = jnp.mean(c * c, axis=-1, keepdims=True)
    return c * lax.rsqrt(var + LN_EPS) * g + b


def _sigmoid(x):
    return 1.0 / (1.0 + jnp.exp(-x))


def _softplus(x):
    return jnp.maximum(x, 0.0) + jnp.log1p(jnp.exp(-jnp.abs(x)))


def _head_ones():
    r = lax.broadcasted_iota(jnp.int32, (LANES, LANES), 0)
    c = lax.broadcasted_iota(jnp.int32, (LANES, LANES), 1)
    return ((r // RW_HEAD) == (c // RW_HEAD)).astype(BF16)


def _head_sum(x, ones):
    hi, lo = _split_bf16(x)
    outs = []
    for j in range(x.shape[1] // LANES):
        sl = slice(j * LANES, (j + 1) * LANES)
        outs.append(_dot(hi[:, sl], ones) + _dot(lo[:, sl], ones))
    return jnp.concatenate(outs, axis=1)


def _inproj_kernel(x_ref, w_ref, u_ref, qkv_ref):
    xb = x_ref[...].astype(BF16)
    u_ref[...] = _dot(xb, w_ref[:, :POOL_WIDTH])
    qkv_ref[...] = _dot(xb, w_ref[:, POOL_WIDTH:]).astype(BF16)


def _inproj(x, w_in, *, tm):
    n = x.shape[0]
    return pl.pallas_call(
        _inproj_kernel,
        grid=(n // tm,),
        in_specs=[pl.BlockSpec((tm, D_MODEL), lambda i: (i, 0)),
                  pl.BlockSpec(w_in.shape, lambda i: (0, 0))],
        out_specs=[pl.BlockSpec((tm, POOL_WIDTH), lambda i: (i, 0)),
                   pl.BlockSpec((tm, 3 * SB_WIDTH), lambda i: (i, 0))],
        out_shape=[jax.ShapeDtypeStruct((n, POOL_WIDTH), F32),
                   jax.ShapeDtypeStruct((n, 3 * SB_WIDTH), BF16)],
        compiler_params=pltpu.CompilerParams(dimension_semantics=("parallel",)),
        name="inproj",
    )(x, w_in)


def _sb_kernel(q_ref, k_ref, v_ref, o_ref, *, t):
    i = pl.program_id(1)
    q = q_ref[...]
    lane = lax.broadcasted_iota(jnp.int32, (t, LANES), 1)
    row = lax.broadcasted_iota(jnp.int32, (t, t), 0)
    col = lax.broadcasted_iota(jnp.int32, (t, t), 1)
    later = (row > col).astype(BF16)
    past = col < row
    scale = SB_HEAD_DIM ** -0.5

    def tile(qh, kb, vb, carry, masked):
        z = _dot_nt(qh, kb) * scale
        sp = _softplus(-z)
        log_beta = -sp
        log_fail = -sp - z
        if masked:
            log_fail = jnp.where(past, log_fail, 0.0)
        hi, lo = _split_bf16(log_fail)
        between = _dot(hi, later) + _dot(lo, later) + carry
        w = jnp.exp(log_beta + between)
        if masked:
            w = jnp.where(past, w, 0.0)
        pv = _dot(w.astype(BF16), vb)
        return pv, carry + jnp.sum(log_fail, axis=1, keepdims=True)

    outs = []
    for h in range(2):
        in_head = (lane >= SB_HEAD_DIM) if h else (lane < SB_HEAD_DIM)
        qh = jnp.where(in_head, q, jnp.zeros_like(q))
        d0 = pl.multiple_of(i * t, t)
        acc, carry = tile(qh, k_ref[pl.ds(d0, t), :], v_ref[pl.ds(d0, t), :],
                          jnp.zeros((t, 1), F32), True)

        def cond(st):
            j, live, _, _ = st
            return jnp.logical_and(j >= 0, live > -EXP_ZERO_BOUND)

        def body(st, qh=qh):
            j, _, carry, acc = st
            s0 = pl.multiple_of(j * t, t)
            pv, carry = tile(qh, k_ref[pl.ds(s0, t), :], v_ref[pl.ds(s0, t), :], carry, False)
            return j - 1, jnp.max(carry), carry, acc + pv

        _, _, _, acc = lax.while_loop(cond, body, (i - 1, jnp.max(carry), carry, acc))
        outs.append(acc)
    o_ref[...] = jnp.where(lane < SB_HEAD_DIM, outs[0], outs[1]).astype(BF16)


def _stick_breaking(qkv, *, t):
    n = qkv.shape[0]
    n_pairs = SB_WIDTH // LANES
    return pl.pallas_call(
        functools.partial(_sb_kernel, t=t),
        grid=(n_pairs, n // t),
        in_specs=[pl.BlockSpec((t, LANES), lambda p, i: (i, p)),
                  pl.BlockSpec((n, LANES), lambda p, i: (0, n_pairs + p)),
                  pl.BlockSpec((n, LANES), lambda p, i: (0, 2 * n_pairs + p))],
        out_specs=pl.BlockSpec((t, LANES), lambda p, i: (i, p)),
        out_shape=jax.ShapeDtypeStruct((n, SB_WIDTH), BF16),
        compiler_params=pltpu.CompilerParams(dimension_semantics=("parallel", "parallel")),
        name="stick_breaking",
    )(qkv, qkv, qkv)


def _mix0_kernel(u_ref, halo_ref, ysb_ref, x_ref, pw_ref, ps_ref, wo_ref, g_ref, b_ref, o_ref, ext_ref, *, tb):
    i = pl.program_id(0)
    u = u_ref[...]
    ext_ref[0:POOL_HALO, :] = jnp.where(i > 0, halo_ref[...], 0.0)
    ext_ref[POOL_HALO:, :] = u
    t_glob = i * tb + lax.broadcasted_iota(jnp.int32, (tb, 1), 0)
    parts = []
    for g, win in enumerate(POOL_WINDOWS):
        cols = slice(g * POOL_GROUP, (g + 1) * POOL_GROUP)
        s = u[:, cols]
        for j in range(1, win):
            s = s + ext_ref[POOL_HALO - j:POOL_HALO - j + tb, cols]
        cnt = jnp.minimum(t_glob + 1, win).astype(F32)
        diff = s / cnt - u[:, cols]
        parts.append(_dot(diff.astype(BF16), pw_ref[g]))
    y_pool = jnp.concatenate(parts, axis=1) * ps_ref[...]
    mix = _dot(y_pool.astype(BF16), wo_ref[:POOL_WIDTH, :]) + _dot(ysb_ref[...], wo_ref[POOL_WIDTH:, :])
    o_ref[...] = _layer_norm(DN_ALPHA * x_ref[...] + mix, g_ref[...], b_ref[...])


def _mix0(u, y_sb, x, pool_w, pool_scale, w_out, ln_g, ln_b, *, tb):
    n = x.shape[0]
    halo_blocks = tb // POOL_HALO
    row = lambda i: (i, 0)
    full = lambda a: pl.BlockSpec(a.shape, lambda i: (0,) * a.ndim)
    return pl.pallas_call(
        functools.partial(_mix0_kernel, tb=tb),
        grid=(n // tb,),
        in_specs=[pl.BlockSpec((tb, POOL_WIDTH), row),
                  pl.BlockSpec((POOL_HALO, POOL_WIDTH), lambda i: (jnp.maximum(i * halo_blocks - 1, 0), 0)),
                  pl.BlockSpec((tb, SB_WIDTH), row),
                  pl.BlockSpec((tb, D_MODEL), row),
                  full(pool_w), full(pool_scale), full(w_out), full(ln_g), full(ln_b)],
        out_specs=pl.BlockSpec((tb, D_MODEL), row),
        out_shape=jax.ShapeDtypeStruct((n, D_MODEL), F32),
        scratch_shapes=[pltpu.VMEM((tb + POOL_HALO, POOL_WIDTH), F32)],
        compiler_params=pltpu.CompilerParams(dimension_semantics=("parallel",)),
        name="pool_outproj_ln",
    )(u, u, y_sb, x, pool_w, pool_scale, w_out, ln_g, ln_b)


def _first_argmax_hit(v, idx, size):
    m = jnp.max(v, axis=0, keepdims=True)
    first = jnp.min(jnp.where(v == m, idx, size), axis=0, keepdims=True)
    return idx == first


def _router_kernel(x_ref, wrt_ref, eb_ref, gates_ref):
    tb = x_ref.shape[0]
    logits = _dot_nt(wrt_ref[...], x_ref[...], precision=_HIGHEST)
    scores = _sigmoid(logits)
    biased = scores + eb_ref[...]
    neg_inf = jnp.float32(-jnp.inf)

    sub = lax.broadcasted_iota(jnp.int32, (GROUP_SIZE, tb), 0)
    group_scores = []
    for g in range(N_GROUPS):
        bg = biased[g * GROUP_SIZE:(g + 1) * GROUP_SIZE, :]
        m1 = jnp.max(bg, axis=0, keepdims=True)
        rest = jnp.where(_first_argmax_hit(bg, sub, GROUP_SIZE), neg_inf, bg)
        group_scores.append(m1 + jnp.max(rest, axis=0, keepdims=True))
    gs = jnp.concatenate(group_scores, axis=0)

    g_idx = lax.broadcasted_iota(jnp.int32, (N_GROUPS, tb), 0)
    group_on = jnp.zeros((N_GROUPS, tb), jnp.bool_)
    for _ in range(TOPK_GROUPS):
        hit = _first_argmax_hit(gs, g_idx, N_GROUPS)
        group_on = jnp.logical_or(group_on, hit)
        gs = jnp.where(hit, neg_inf, gs)

    masked = jnp.concatenate(
        [jnp.where(group_on[g:g + 1, :], biased[g * GROUP_SIZE:(g + 1) * GROUP_SIZE, :], neg_inf)
         for g in range(N_GROUPS)], axis=0)
    e_idx = lax.broadcasted_iota(jnp.int32, (N_EXPERTS, tb), 0)
    chosen = jnp.zeros((N_EXPERTS, tb), jnp.bool_)
    for _ in range(TOP_K):
        hit = _first_argmax_hit(masked, e_idx, N_EXPERTS)
        chosen = jnp.logical_or(chosen, hit)
        masked = jnp.where(hit, neg_inf, masked)

    sel = jnp.where(chosen, scores, 0.0)
    gates_ref[...] = sel / jnp.sum(sel, axis=0, keepdims=True) * ROUTED_SCALE


def _router(x, w_router_t, e_bias_col, *, tb):
    n = x.shape[0]
    return pl.pallas_call(
        _router_kernel,
        grid=(n // tb,),
        in_specs=[pl.BlockSpec((tb, D_MODEL), lambda i: (i, 0)),
                  pl.BlockSpec(w_router_t.shape, lambda i: (0, 0)),
                  pl.BlockSpec(e_bias_col.shape, lambda i: (0, 0))],
        out_specs=pl.BlockSpec((N_EXPERTS, tb), lambda i: (0, i)),
        out_shape=jax.ShapeDtypeStruct((N_EXPERTS, n), F32),
        compiler_params=pltpu.CompilerParams(dimension_semantics=("parallel",)),
        name="router",
    )(x, w_router_t, e_bias_col)


def _swiglu_hidden(xb, wg, wu):
    hg = _dot(xb, wg)
    return hg * _sigmoid(hg) * _dot(xb, wu)


def _moe_kernel(x_ref, gates_ref, wg_ref, wu_ref, wd_ref, sg_ref, su_ref, sd_ref, g_ref, b_ref, o_ref,
                xb_ref, ghl_ref, acc_ref):
    e = pl.program_id(1)

    @pl.when(e == 0)
    def _():
        xb = x_ref[...].astype(BF16)
        xb_ref[...] = xb
        hi, lo = _split_bf16(gates_ref[...])
        ghl_ref[...] = jnp.concatenate([hi, lo], axis=1)
        acc_ref[...] = _dot(_swiglu_hidden(xb, sg_ref[...], su_ref[...]).astype(BF16), sd_ref[...])

    xb = xb_ref[...]
    r = lax.broadcasted_iota(jnp.int32, (2 * N_EXPERTS, EXPERT_FF), 0)
    pick = jnp.logical_or(r == e, r == e + N_EXPERTS).astype(BF16)
    gate = _dot(ghl_ref[...], pick)
    act = _swiglu_hidden(xb, wg_ref[...], wu_ref[...]) * gate
    acc_ref[...] += _dot(act.astype(BF16), wd_ref[...])

    @pl.when(e == pl.num_programs(1) - 1)
    def _():
        o_ref[...] = _layer_norm(DN_ALPHA * x_ref[...] + acc_ref[...], g_ref[...], b_ref[...])


def _moe(x, gates, wg, wu, wd, sg, su, sd, ln_g, ln_b, *, tb):
    n = x.shape[0]
    tok = lambda i, e: (i, 0)
    exp = lambda i, e: (e, 0, 0)
    full = lambda a: pl.BlockSpec(a.shape, lambda i, e: (0,) * a.ndim)
    return pl.pallas_call(
        _moe_kernel,
        grid=(n // tb, N_EXPERTS),
        in_specs=[pl.BlockSpec((tb, D_MODEL), tok),
                  pl.BlockSpec((tb, N_EXPERTS), tok),
                  pl.BlockSpec((None, D_MODEL, EXPERT_FF), exp),
                  pl.BlockSpec((None, D_MODEL, EXPERT_FF), exp),
                  pl.BlockSpec((None, EXPERT_FF, D_MODEL), exp),
                  full(sg), full(su), full(sd), full(ln_g), full(ln_b)],
        out_specs=pl.BlockSpec((tb, D_MODEL), tok),
        out_shape=jax.ShapeDtypeStruct((n, D_MODEL), F32),
        scratch_shapes=[pltpu.VMEM((tb, D_MODEL), BF16),
                        pltpu.VMEM((tb, 2 * N_EXPERTS), BF16),
                        pltpu.VMEM((tb, D_MODEL), F32)],
        compiler_params=pltpu.CompilerParams(dimension_semantics=("parallel", "arbitrary")),
        name="moe_experts_ln",
    )(x, gates, wg, wu, wd, sg, su, sd, ln_g, ln_b)


def _rwkv_pre_kernel(x_ref, prev_ref, mu_ref, wr_ref, wk_ref, wv_ref, w1_ref, w2_ref, w0_ref,
                     a1_ref, a2_ref, a0_ref, g1_ref, g2_ref, kk_scale_ref, ka_ref,
                     r_ref, d_ref, k_ref, v_ref, kk_ref, b_ref, g_ref, ext_ref, *, tb):
    i = pl.program_id(0)
    x = x_ref[...]
    ext_ref[0:SUBLANES, :] = jnp.where(i > 0, prev_ref[...], 0.0)
    ext_ref[SUBLANES:, :] = x
    xx = ext_ref[SUBLANES - 1:SUBLANES - 1 + tb, :] - x
    mix = lambda j: (x + xx * mu_ref[j:j + 1, :]).astype(BF16)
    xr, xw, xk, xv, xa, xg = [mix(j) for j in range(6)]

    r_ref[...] = _dot(xr, wr_ref[...])
    w = w0_ref[...] + _dot(jnp.tanh(_dot(xw, w1_ref[...])).astype(BF16), w2_ref[...])
    w_log = -_softplus(-w) - 0.5
    d_ref[...] = jnp.exp(-jnp.exp(w_log))
    k = _dot(xk, wk_ref[...])
    v_ref[...] = _dot(xv, wv_ref[...])
    a = _sigmoid(a0_ref[...] + _dot(_dot(xa, a1_ref[...]).astype(BF16), a2_ref[...]))
    g_ref[...] = _dot(_sigmoid(_dot(xg, g1_ref[...])).astype(BF16), g2_ref[...])

    kk = k * kk_scale_ref[...]
    norm = jnp.sqrt(_head_sum(kk * kk, _head_ones()))
    kk = kk / jnp.maximum(norm, 1e-12)
    kk_ref[...] = kk
    b_ref[...] = kk * a
    k_ref[...] = k * (1.0 + (a - 1.0) * ka_ref[...])


def _rwkv_pre(x, mu, wr, wk, wv, w1, w2, w0, a1, a2, a0, g1, g2, k_k, k_a, *, tb):
    n = x.shape[0]
    row = lambda i: (i, 0)
    full = lambda a: pl.BlockSpec(a.shape, lambda i: (0,) * a.ndim)
    params = (mu, wr, wk, wv, w1, w2, w0, a1, a2, a0, g1, g2, k_k, k_a)
    return pl.pallas_call(
        functools.partial(_rwkv_pre_kernel, tb=tb),
        grid=(n // tb,),
        in_specs=[pl.BlockSpec((tb, D_MODEL), row),
                  pl.BlockSpec((SUBLANES, D_MODEL), lambda i: (jnp.maximum(i * (tb // SUBLANES) - 1, 0), 0))]
                 + [full(p) for p in params],
        out_specs=[pl.BlockSpec((tb, D_MODEL), row)] * 7,
        out_shape=[jax.ShapeDtypeStruct((n, D_MODEL), F32)] * 7,
        scratch_shapes=[pltpu.VMEM((tb + SUBLANES, D_MODEL), F32)],
        compiler_params=pltpu.CompilerParams(dimension_semantics=("parallel",)),
        name="rwkv_projections",
    )(x, x, *params)


SCAN_SUB = 64
N_PAIRS = D_MODEL // LANES


def _pair_time_tile(x_sub):
    lane = lax.broadcasted_iota(jnp.int32, (SCAN_SUB, LANES), 1)
    top = jnp.where(lane < RW_HEAD, x_sub, 0.0)
    bot = jnp.where(lane >= RW_HEAD, x_sub, 0.0)
    bdt = jnp.concatenate([top, bot], axis=0).T
    return bdt[0:RW_HEAD, :] + bdt[RW_HEAD:, :]


def _scan_kernel(r_ref, d_ref, k_ref, v_ref, kk_ref, b_ref, y_ref, s_ref, z_ref, *, tb):
    @pl.when(pl.program_id(0) == 0)
    def _():
        s_ref[...] = jnp.zeros_like(s_ref)

    lane = lax.broadcasted_iota(jnp.int32, (RW_HEAD, LANES), 1)
    half = jnp.where(lane >= RW_HEAD, RW_HEAD, 0)

    def sub_block(sb, carry):
        row0 = pl.multiple_of(sb * SCAN_SUB, SCAN_SUB)
        for vi, ref in enumerate((kk_ref, d_ref, b_ref, k_ref, r_ref)):
            for p in range(N_PAIRS):
                z_ref[vi, p] = _pair_time_tile(ref[pl.ds(row0, SCAN_SUB), p * LANES:(p + 1) * LANES])

        def steps(g, c):
            rowg = pl.multiple_of(row0 + g * SUBLANES, SUBLANES)
            ys = [[None] * SUBLANES for _ in range(N_PAIRS)]
            vblk = [v_ref[pl.ds(rowg, SUBLANES), p * LANES:(p + 1) * LANES] for p in range(N_PAIRS)]
            for j in range(SUBLANES):
                idx = half + (g * SUBLANES + j)
                for p in range(N_PAIRS):
                    col = lambda vi: jnp.take_along_axis(z_ref[vi, p], idx, axis=1)
                    s = s_ref[p]
                    sa = jnp.sum(s * col(0), axis=0, keepdims=True)
                    s2 = s * col(1) - col(2) * sa + col(3) * vblk[p][j:j + 1, :]
                    ys[p][j] = jnp.sum(s2 * col(4), axis=0, keepdims=True)
                    s_ref[p] = s2
            for p in range(N_PAIRS):
                y_ref[pl.ds(rowg, SUBLANES), p * LANES:(p + 1) * LANES] = jnp.concatenate(ys[p], axis=0)
            return c

        lax.fori_loop(0, SCAN_SUB // SUBLANES, steps, 0)
        return carry

    lax.fori_loop(0, tb // SCAN_SUB, sub_block, 0)


def _rwkv_scan(r, d, k, v, kk, b, *, tb):
    n = r.shape[0]
    spec = pl.BlockSpec((tb, D_MODEL), lambda i: (i, 0))
    return pl.pallas_call(
        functools.partial(_scan_kernel, tb=tb),
        grid=(n // tb,),
        in_specs=[spec] * 6,
        out_specs=spec,
        out_shape=jax.ShapeDtypeStruct((n, D_MODEL), F32),
        scratch_shapes=[pltpu.VMEM((N_PAIRS, RW_HEAD, LANES), F32),
                        pltpu.VMEM((5, N_PAIRS, RW_HEAD, LANES), F32)],
        compiler_params=pltpu.CompilerParams(dimension_semantics=("arbitrary",)),
        name="rwkv_scan",
    )(r, d, k, v, kk, b)


def _rwkv_post_kernel(y_ref, r_ref, k_ref, v_ref, g_ref, x_ref, rk_ref, lg_ref, lb_ref, wo_ref,
                      ng_ref, nb_ref, o_ref):
    ones = _head_ones()
    y = y_ref[...]
    inv_n = 1.0 / RW_HEAD
    c = y - _head_sum(y, ones) * inv_n
    var = _head_sum(c * c, ones) * inv_n
    yn = c * lax.rsqrt(var + GN_EPS) * lg_ref[...] + lb_ref[...]
    bonus = _head_sum(r_ref[...] * k_ref[...] * rk_ref[...], ones) * v_ref[...]
    out = _dot(((yn + bonus) * g_ref[...]).astype(BF16), wo_ref[...])
    o_ref[...] = _layer_norm(DN_ALPHA * x_ref[...] + out, ng_ref[...], nb_ref[...])


def _rwkv_post(y, r, k, v, g, x, r_k, lnx_g, lnx_b, w_o, ln_g, ln_b, *, tb):
    n = x.shape[0]
    row = pl.BlockSpec((tb, D_MODEL), lambda i: (i, 0))
    full = lambda a: pl.BlockSpec(a.shape, lambda i: (0,) * a.ndim)
    params = (r_k, lnx_g, lnx_b, w_o, ln_g, ln_b)
    return pl.pallas_call(
        _rwkv_post_kernel,
        grid=(n // tb,),
        in_specs=[row] * 6 + [full(p) for p in params],
        out_specs=row,
        out_shape=jax.ShapeDtypeStruct((n, D_MODEL), F32),
        compiler_params=pltpu.CompilerParams(dimension_semantics=("parallel",)),
        name="rwkv_output_ln",
    )(y, r, k, v, g, x, *params)


def _row(a):
    return a.reshape(1, -1)


def _moe_layer(x, layer, w_router, e_bias, w_gate, w_up, w_down, ws_gate, ws_up, ws_down, ln_g, ln_b):
    n = x.shape[0]
    gates_t = _router(x, w_router[layer].T, e_bias[layer].reshape(-1, 1), tb=min(512, n))
    return _moe(x, gates_t.T,
                w_gate[layer].astype(BF16), w_up[layer].astype(BF16), w_down[layer].astype(BF16),
                ws_gate[layer].astype(BF16), ws_up[layer].astype(BF16), ws_down[layer].astype(BF16),
                _row(ln_g[layer]), _row(ln_b[layer]), tb=min(1024, n))


def kernel(x, ln_mix_g, ln_mix_b, ln_ffn_g, ln_ffn_b, ab_w_in, ab_pool_w, ab_pool_scale, ab_w_out, rw_mu, rw_w_r, rw_w_k, rw_w_v, rw_w_w1, rw_w_w2, rw_w0, rw_a1, rw_a2, rw_a0, rw_g1, rw_g2, rw_k_k, rw_k_a, rw_r_k, rw_lnx_g, rw_lnx_b, rw_w_o, moe_w_router, moe_e_bias, moe_w_gate, moe_w_up, moe_w_down, moe_ws_gate, moe_ws_up, moe_ws_down):
    batch, seq, d = x.shape
    assert batch == 1 and d == D_MODEL and seq % SCAN_SUB == 0
    n = seq
    h = x.reshape(n, d)
    moe_args = (moe_w_router, moe_e_bias, moe_w_gate, moe_w_up, moe_w_down, moe_ws_gate, moe_ws_up, moe_ws_down)

    u, qkv = _inproj(h, ab_w_in[0].astype(BF16), tm=min(512, n))
    y_sb = _stick_breaking(qkv, t=min(128, n))
    h = _mix0(u, y_sb, h, ab_pool_w[0].astype(BF16), _row(ab_pool_scale[0]), ab_w_out[0].astype(BF16),
              _row(ln_mix_g[0]), _row(ln_mix_b[0]), tb=min(256, n))
    h = _moe_layer(h, 0, *moe_args, ln_ffn_g, ln_ffn_b)

    bf = lambda a: a[0].astype(BF16)
    r, dec, k, v, kk, b, g = _rwkv_pre(
        h, rw_mu[0], bf(rw_w_r), bf(rw_w_k), bf(rw_w_v), bf(rw_w_w1), bf(rw_w_w2), _row(rw_w0[0]),
        bf(rw_a1), bf(rw_a2), _row(rw_a0[0]), bf(rw_g1), bf(rw_g2), _row(rw_k_k[0]), _row(rw_k_a[0]),
        tb=min(256, n))
    y = _rwkv_scan(r, dec, k, v, kk, b, tb=min(256, n))
    h = _rwkv_post(y, r, k, v, g, h, _row(rw_r_k[0]), _row(rw_lnx_g[0]), _row(rw_lnx_b[0]), bf(rw_w_o),
                   _row(ln_mix_g[1]), _row(ln_mix_b[1]), tb=min(256, n))
    h = _moe_layer(h, 1, *moe_args, ln_ffn_g, ln_ffn_b)
    return h.reshape(batch, seq, d)
```

```python
import functools

import jax
import jax.numpy as jnp
from jax import lax
from jax.experimental import pallas as pl
from jax.experimental.pallas import tpu as pltpu

F32 = jnp.float32
BF16 = jnp.bfloat16

D_MODEL = 1024
DEPTH = 2
POOL_WINDOWS = (2, 4, 8, 16)
POOL_WIDTH = 512
POOL_GROUP = 128
POOL_HALO = 16
SB_HEAD_DIM = 64
SB_WIDTH = 512
RW_HEAD = 64
GN_EPS = 64e-5
N_EXPERTS = 64
TOP_K = 8
N_GROUPS = 8
GROUP_SIZE = N_EXPERTS // N_GROUPS
TOPK_GROUPS = 4
EXPERT_FF = 256
ROUTED_SCALE = 2.5
DN_ALPHA = (2.0 * DEPTH) ** 0.25
LN_EPS = 1e-5

LANES = 128
SUBLANES = 8
EXP_ZERO_BOUND = 110.0

_HIGHEST = lax.Precision.HIGHEST


def _dot(a, b, **kw):
    return jnp.dot(a, b, preferred_element_type=F32, **kw)


def _dot_nt(a, b, **kw):
    return lax.dot_general(a, b, (((1,), (1,)), ((), ())), preferred_element_type=F32, **kw)


def _split_bf16(x):
    hi = x.astype(BF16)
    lo = (x - hi.astype(F32)).astype(BF16)
    return hi, lo


def _layer_norm(h, g, b):
    mu = jnp.mean(h, axis=-1, keepdims=True)
    c = h - mu
    var = jnp.mean(c * c, axis=-1, keepdims=True)
    return c * lax.rsqrt(var + LN_EPS) * g + b


def _sigmoid(x):
    return 1.0 / (1.0 + jnp.exp(-x))


def _softplus(x):
    return jnp.maximum(x, 0.0) + jnp.log1p(jnp.exp(-jnp.abs(x)))


def _head_ones():
    r = lax.broadcasted_iota(jnp.int32, (LANES, LANES), 0)
    c = lax.broadcasted_iota(jnp.int32, (LANES, LANES), 1)
    return ((r // RW_HEAD) == (c // RW_HEAD)).astype(BF16)


def _head_sum(x, ones):
    hi, lo = _split_bf16(x)
    outs = []
    for j in range(x.shape[1] // LANES):
        sl = slice(j * LANES, (j + 1) * LANES)
        outs.append(_dot(hi[:, sl], ones) + _dot(lo[:, sl], ones))
    return jnp.concatenate(outs, axis=1)


def _inproj_kernel(x_ref, w_ref, u_ref, qkv_ref):
    xb = x_ref[...].astype(BF16)
    u_ref[...] = _dot(xb, w_ref[:, :POOL_WIDTH])
    qkv_ref[...] = _dot(xb, w_ref[:, POOL_WIDTH:]).astype(BF16)


def _inproj(x, w_in, *, tm):
    n = x.shape[0]
    return pl.pallas_call(
        _inproj_kernel,
        grid=(n // tm,),
        in_specs=[pl.BlockSpec((tm, D_MODEL), lambda i: (i, 0)),
                  pl.BlockSpec(w_in.shape, lambda i: (0, 0))],
        out_specs=[pl.BlockSpec((tm, POOL_WIDTH), lambda i: (i, 0)),
                   pl.BlockSpec((tm, 3 * SB_WIDTH), lambda i: (i, 0))],
        out_shape=[jax.ShapeDtypeStruct((n, POOL_WIDTH), F32),
                   jax.ShapeDtypeStruct((n, 3 * SB_WIDTH), BF16)],
        compiler_params=pltpu.CompilerParams(dimension_semantics=("parallel",)),
        name="inproj",
    )(x, w_in)


def _sb_kernel(q_ref, k_ref, v_ref, o_ref, *, t):
    i = pl.program_id(1)
    q = q_ref[...]
    lane = lax.broadcasted_iota(jnp.int32, (t, LANES), 1)
    row = lax.broadcasted_iota(jnp.int32, (t, t), 0)
    col = lax.broadcasted_iota(jnp.int32, (t, t), 1)
    later = (row > col).astype(BF16)
    past = col < row
    scale = SB_HEAD_DIM ** -0.5

    def tile(qh, kb, vb, carry, masked):
        z = _dot_nt(qh, kb) * scale
        sp = _softplus(-z)
        log_beta = -sp
        log_fail = -sp - z
        if masked:
            log_fail = jnp.where(past, log_fail, 0.0)
        hi, lo = _split_bf16(log_fail)
        between = _dot(hi, later) + _dot(lo, later) + carry
        w = jnp.exp(log_beta + between)
        if masked:
            w = jnp.where(past, w, 0.0)
        pv = _dot(w.astype(BF16), vb)
        return pv, carry + jnp.sum(log_fail, axis=1, keepdims=True)

    outs = []
    for h in range(2):
        in_head = (lane >= SB_HEAD_DIM) if h else (lane < SB_HEAD_DIM)
        qh = jnp.where(in_head, q, jnp.zeros_like(q))
        d0 = pl.multiple_of(i * t, t)
        acc, carry = tile(qh, k_ref[pl.ds(d0, t), :], v_ref[pl.ds(d0, t), :],
                          jnp.zeros((t, 1), F32), True)

        def cond(st):
            j, live, _, _ = st
            return jnp.logical_and(j >= 0, live > -EXP_ZERO_BOUND)

        def body(st, qh=qh):
            j, _, carry, acc = st
            s0 = pl.multiple_of(j * t, t)
            pv, carry = tile(qh, k_ref[pl.ds(s0, t), :], v_ref[pl.ds(s0, t), :], carry, False)
            return j - 1, jnp.max(carry), carry, acc + pv

        _, _, _, acc = lax.while_loop(cond, body, (i - 1, jnp.max(carry), carry, acc))
        outs.append(acc)
    o_ref[...] = jnp.where(lane < SB_HEAD_DIM, outs[0], outs[1]).astype(BF16)


def _stick_breaking(qkv, *, t):
    n = qkv.shape[0]
    n_pairs = SB_WIDTH // LANES
    return pl.pallas_call(
        functools.partial(_sb_kernel, t=t),
        grid=(n_pairs, n // t),
        in_specs=[pl.BlockSpec((t, LANES), lambda p, i: (i, p)),
                  pl.BlockSpec((n, LANES), lambda p, i: (0, n_pairs + p)),
                  pl.BlockSpec((n, LANES), lambda p, i: (0, 2 * n_pairs + p))],
        out_specs=pl.BlockSpec((t, LANES), lambda p, i: (i, p)),
        out_shape=jax.ShapeDtypeStruct((n, SB_WIDTH), BF16),
        compiler_params=pltpu.CompilerParams(dimension_semantics=("parallel", "parallel")),
        name="stick_breaking",
    )(qkv, qkv, qkv)


def _mix0_kernel(u_ref, halo_ref, ysb_ref, x_ref, pw_ref, ps_ref, wo_ref, g_ref, b_ref, o_ref, ext_ref, *, tb):
    i = pl.program_id(0)
    u = u_ref[...]
    ext_ref[0:POOL_HALO, :] = jnp.where(i > 0, halo_ref[...], 0.0)
    ext_ref[POOL_HALO:, :] = u
    t_glob = i * tb + lax.broadcasted_iota(jnp.int32, (tb, 1), 0)
    parts = []
    for g, win in enumerate(POOL_WINDOWS):
        cols = slice(g * POOL_GROUP, (g + 1) * POOL_GROUP)
        s = u[:, cols]
        for j in range(1, win):
            s = s + ext_ref[POOL_HALO - j:POOL_HALO - j + tb, cols]
        cnt = jnp.minimum(t_glob + 1, win).astype(F32)
        diff = s / cnt - u[:, cols]
        parts.append(_dot(diff.astype(BF16), pw_ref[g]))
    y_pool = jnp.concatenate(parts, axis=1) * ps_ref[...]
    mix = _dot(y_pool.astype(BF16), wo_ref[:POOL_WIDTH, :]) + _dot(ysb_ref[...], wo_ref[POOL_WIDTH:, :])
    o_ref[...] = _layer_norm(DN_ALPHA * x_ref[...] + mix, g_ref[...], b_ref[...])


def _mix0(u, y_sb, x, pool_w, pool_scale, w_out, ln_g, ln_b, *, tb):
    n = x.shape[0]
    halo_blocks = tb // POOL_HALO
    row = lambda i: (i, 0)
    full = lambda a: pl.BlockSpec(a.shape, lambda i: (0,) * a.ndim)
    return pl.pallas_call(
        functools.partial(_mix0_kernel, tb=tb),
        grid=(n // tb,),
        in_specs=[pl.BlockSpec((tb, POOL_WIDTH), row),
                  pl.BlockSpec((POOL_HALO, POOL_WIDTH), lambda i: (jnp.maximum(i * halo_blocks - 1, 0), 0)),
                  pl.BlockSpec((tb, SB_WIDTH), row),
                  pl.BlockSpec((tb, D_MODEL), row),
                  full(pool_w), full(pool_scale), full(w_out), full(ln_g), full(ln_b)],
        out_specs=pl.BlockSpec((tb, D_MODEL), row),
        out_shape=jax.ShapeDtypeStruct((n, D_MODEL), F32),
        scratch_shapes=[pltpu.VMEM((tb + POOL_HALO, POOL_WIDTH), F32)],
        compiler_params=pltpu.CompilerParams(dimension_semantics=("parallel",)),
        name="pool_outproj_ln",
    )(u, u, y_sb, x, pool_w, pool_scale, w_out, ln_g, ln_b)


def _first_argmax_hit(v, idx, size):
    m = jnp.max(v, axis=0, keepdims=True)
    first = jnp.min(jnp.where(v == m, idx, size), axis=0, keepdims=True)
    return idx == first


def _router_kernel(x_ref, wrt_ref, eb_ref, gates_ref):
    tb = x_ref.shape[0]
    logits = _dot_nt(wrt_ref[...], x_ref[...], precision=_HIGHEST)
    scores = _sigmoid(logits)
    biased = scores + eb_ref[...]
    neg_inf = jnp.float32(-jnp.inf)

    sub = lax.broadcasted_iota(jnp.int32, (GROUP_SIZE, tb), 0)
    group_scores = []
    for g in range(N_GROUPS):
        bg = biased[g * GROUP_SIZE:(g + 1) * GROUP_SIZE, :]
        m1 = jnp.max(bg, axis=0, keepdims=True)
        rest = jnp.where(_first_argmax_hit(bg, sub, GROUP_SIZE), neg_inf, bg)
        group_scores.append(m1 + jnp.max(rest, axis=0, keepdims=True))
    gs = jnp.concatenate(group_scores, axis=0)

    g_idx = lax.broadcasted_iota(jnp.int32, (N_GROUPS, tb), 0)
    group_on = jnp.zeros((N_GROUPS, tb), jnp.bool_)
    for _ in range(TOPK_GROUPS):
        hit = _first_argmax_hit(gs, g_idx, N_GROUPS)
        group_on = jnp.logical_or(group_on, hit)
        gs = jnp.where(hit, neg_inf, gs)

    masked = jnp.concatenate(
        [jnp.where(group_on[g:g + 1, :], biased[g * GROUP_SIZE:(g + 1) * GROUP_SIZE, :], neg_inf)
         for g in range(N_GROUPS)], axis=0)
    e_idx = lax.broadcasted_iota(jnp.int32, (N_EXPERTS, tb), 0)
    chosen = jnp.zeros((N_EXPERTS, tb), jnp.bool_)
    for _ in range(TOP_K):
        hit = _first_argmax_hit(masked, e_idx, N_EXPERTS)
        chosen = jnp.logical_or(chosen, hit)
        masked = jnp.where(hit, neg_inf, masked)

    sel = jnp.where(chosen, scores, 0.0)
    gates_ref[...] = sel / jnp.sum(sel, axis=0, keepdims=True) * ROUTED_SCALE


def _router(x, w_router_t, e_bias_col, *, tb):
    n = x.shape[0]
    return pl.pallas_call(
        _router_kernel,
        grid=(n // tb,),
        in_specs=[pl.BlockSpec((tb, D_MODEL), lambda i: (i, 0)),
                  pl.BlockSpec(w_router_t.shape, lambda i: (0, 0)),
                  pl.BlockSpec(e_bias_col.shape, lambda i: (0, 0))],
        out_specs=pl.BlockSpec((N_EXPERTS, tb), lambda i: (0, i)),
        out_shape=jax.ShapeDtypeStruct((N_EXPERTS, n), F32),
        compiler_params=pltpu.CompilerParams(dimension_semantics=("parallel",)),
        name="router",
    )(x, w_router_t, e_bias_col)


def _swiglu_hidden(xb, wg, wu):
    hg = _dot(xb, wg)
    return hg * _sigmoid(hg) * _dot(xb, wu)


def _moe_kernel(x_ref, gates_ref, wg_ref, wu_ref, wd_ref, sg_ref, su_ref, sd_ref, g_ref, b_ref, o_ref,
                xb_ref, ghl_ref, acc_ref):
    e = pl.program_id(1)

    @pl.when(e == 0)
    def _():
        xb = x_ref[...].astype(BF16)
        xb_ref[...] = xb
        hi, lo = _split_bf16(gates_ref[...])
        ghl_ref[...] = jnp.concatenate([hi, lo], axis=1)
        acc_ref[...] = _dot(_swiglu_hidden(xb, sg_ref[...], su_ref[...]).astype(BF16), sd_ref[...])

    xb = xb_ref[...]
    r = lax.broadcasted_iota(jnp.int32, (2 * N_EXPERTS, EXPERT_FF), 0)
    pick = jnp.logical_or(r == e, r == e + N_EXPERTS).astype(BF16)
    gate = _dot(ghl_ref[...], pick)
    act = _swiglu_hidden(xb, wg_ref[...], wu_ref[...]) * gate
    acc_ref[...] += _dot(act.astype(BF16), wd_ref[...])

    @pl.when(e == pl.num_programs(1) - 1)
    def _():
        o_ref[...] = _layer_norm(DN_ALPHA * x_ref[...] + acc_ref[...], g_ref[...], b_ref[...])


def _moe(x, gates, wg, wu, wd, sg, su, sd, ln_g, ln_b, *, tb):
    n = x.shape[0]
    tok = lambda i, e: (i, 0)
    exp = lambda i, e: (e, 0, 0)
    full = lambda a: pl.BlockSpec(a.shape, lambda i, e: (0,) * a.ndim)
    return pl.pallas_call(
        _moe_kernel,
        grid=(n // tb, N_EXPERTS),
        in_specs=[pl.BlockSpec((tb, D_MODEL), tok),
                  pl.BlockSpec((tb, N_EXPERTS), tok),
                  pl.BlockSpec((None, D_MODEL, EXPERT_FF), exp),
                  pl.BlockSpec((None, D_MODEL, EXPERT_FF), exp),
                  pl.BlockSpec((None, EXPERT_FF, D_MODEL), exp),
                  full(sg), full(su), full(sd), full(ln_g), full(ln_b)],
        out_specs=pl.BlockSpec((tb, D_MODEL), tok),
        out_shape=jax.ShapeDtypeStruct((n, D_MODEL), F32),
        scratch_shapes=[pltpu.VMEM((tb, D_MODEL), BF16),
                        pltpu.VMEM((tb, 2 * N_EXPERTS), BF16),
                        pltpu.VMEM((tb, D_MODEL), F32)],
        compiler_params=pltpu.CompilerParams(dimension_semantics=("parallel", "arbitrary")),
        name="moe_experts_ln",
    )(x, gates, wg, wu, wd, sg, su, sd, ln_g, ln_b)


def _rwkv_pre_kernel(x_ref, prev_ref, mu_ref, wr_ref, wk_ref, wv_ref, w1_ref, w2_ref, w0_ref,
                     a1_ref, a2_ref, a0_ref, g1_ref, g2_ref, kk_scale_ref, ka_ref,
                     r_ref, ld_ref, k_ref, v_ref, kk_ref, b_ref, g_ref, ext_ref, *, tb):
    i = pl.program_id(0)
    x = x_ref[...]
    ext_ref[0:SUBLANES, :] = jnp.where(i > 0, prev_ref[...], 0.0)
    ext_ref[SUBLANES:, :] = x
    xx = ext_ref[SUBLANES - 1:SUBLANES - 1 + tb, :] - x
    mix = lambda j: (x + xx * mu_ref[j:j + 1, :]).astype(BF16)
    xr, xw, xk, xv, xa, xg = [mix(j) for j in range(6)]

    r_ref[...] = _dot(xr, wr_ref[...])
    w = w0_ref[...] + _dot(jnp.tanh(_dot(xw, w1_ref[...])).astype(BF16), w2_ref[...])
    w_log = -_softplus(-w) - 0.5
    ld_ref[...] = -jnp.exp(w_log)
    k = _dot(xk, wk_ref[...])
    v_ref[...] = _dot(xv, wv_ref[...])
    a = _sigmoid(a0_ref[...] + _dot(_dot(xa, a1_ref[...]).astype(BF16), a2_ref[...]))
    g_ref[...] = _dot(_sigmoid(_dot(xg, g1_ref[...])).astype(BF16), g2_ref[...])

    kk = k * kk_scale_ref[...]
    norm = jnp.sqrt(_head_sum(kk * kk, _head_ones()))
    kk = kk / jnp.maximum(norm, 1e-12)
    kk_ref[...] = kk
    b_ref[...] = kk * a
    k_ref[...] = k * (1.0 + (a - 1.0) * ka_ref[...])


def _rwkv_pre(x, mu, wr, wk, wv, w1, w2, w0, a1, a2, a0, g1, g2, k_k, k_a, *, tb):
    n = x.shape[0]
    row = lambda i: (i, 0)
    full = lambda a: pl.BlockSpec(a.shape, lambda i: (0,) * a.ndim)
    params = (mu, wr, wk, wv, w1, w2, w0, a1, a2, a0, g1, g2, k_k, k_a)
    return pl.pallas_call(
        functools.partial(_rwkv_pre_kernel, tb=tb),
        grid=(n // tb,),
        in_specs=[pl.BlockSpec((tb, D_MODEL), row),
                  pl.BlockSpec((SUBLANES, D_MODEL), lambda i: (jnp.maximum(i * (tb // SUBLANES) - 1, 0), 0))]
                 + [full(p) for p in params],
        out_specs=[pl.BlockSpec((tb, D_MODEL), row)] * 7,
        out_shape=[jax.ShapeDtypeStruct((n, D_MODEL), F32)] * 7,
        scratch_shapes=[pltpu.VMEM((tb + SUBLANES, D_MODEL), F32)],
        compiler_params=pltpu.CompilerParams(dimension_semantics=("parallel",)),
        name="rwkv_projections",
    )(x, x, *params)


SCAN_CHUNK = 64
N_PAIRS = D_MODEL // LANES


def _dot_tn(a, b):
    return lax.dot_general(a, b, (((0,), (0,)), ((), ())), preferred_element_type=F32)


def _split3_bf16(x):
    hi = x.astype(BF16)
    r1 = x - hi.astype(F32)
    mid = r1.astype(BF16)
    return hi, mid, (r1 - mid.astype(F32)).astype(BF16)


def _block_diag(x):
    lane = lax.broadcasted_iota(jnp.int32, x.shape, 1)
    zero = jnp.zeros_like(x)
    return jnp.concatenate([jnp.where(lane < RW_HEAD, x, zero), jnp.where(lane >= RW_HEAD, x, zero)], axis=0)


def _same_head_mask():
    r = lax.broadcasted_iota(jnp.int32, (LANES, LANES), 0)
    c = lax.broadcasted_iota(jnp.int32, (LANES, LANES), 1)
    return (r // RW_HEAD) == (c // RW_HEAD)


def _scan_operands_kernel(r_ref, ld_ref, k_ref, v_ref, kk_ref, b_ref,
                          ab_ref, av_ref, kr_ref, arb_ref, bp_ref, kpv_ref, dend_ref, *, tb):
    c = SCAN_CHUNK
    row = lax.broadcasted_iota(jnp.int32, (c, LANES), 0)
    li = lax.broadcasted_iota(jnp.int32, (c, LANES), 1) % RW_HEAD
    strict = li < row
    incl = li <= row
    r2 = lax.broadcasted_iota(jnp.int32, (c, c), 0)
    c2 = lax.broadcasted_iota(jnp.int32, (c, c), 1)
    cumsum = (c2 <= r2).astype(BF16)
    same_head = _same_head_mask()

    def chunk(ci, carry):
        rows = pl.ds(pl.multiple_of(ci * c, c), c)
        rows2 = pl.ds(pl.multiple_of(ci * 2 * c, 2 * c), 2 * c)
        rows8 = pl.ds(pl.multiple_of(ci * SUBLANES, SUBLANES), SUBLANES)
        for p in range(N_PAIRS):
            lanes = slice(p * LANES, (p + 1) * LANES)
            ld_step = ld_ref[rows, lanes]
            hi, mid, lo = _split3_bf16(ld_step)
            ld = _dot(cumsum, hi) + _dot(cumsum, mid) + _dot(cumsum, lo)
            ld_end = ld[c - 1:c, :]
            inv_d = jnp.exp(-ld)
            to_end = jnp.exp(ld_end - ld)
            k = k_ref[rows, lanes]
            b = b_ref[rows, lanes]
            vb = v_ref[rows, lanes].astype(BF16)
            kk_t = kk_ref[rows, lanes] * jnp.exp(ld - ld_step)
            r_t = r_ref[rows, lanes] * jnp.exp(ld)
            kr = jnp.concatenate([kk_t, r_t], axis=0).astype(BF16)
            a_k = _dot_nt(kr, _block_diag((k * inv_d).astype(BF16)))
            a_b = _dot_nt(kr, _block_diag((b * inv_d).astype(BF16)))
            ab_ref[rows, lanes] = jnp.where(strict, a_b[:c], 0.0)
            arb_ref[rows, lanes] = jnp.where(incl, a_b[c:], 0.0).astype(BF16)
            a_k = jnp.concatenate([jnp.where(strict, a_k[:c], 0.0), jnp.where(incl, a_k[c:], 0.0)], axis=0)
            av_ref[rows2, lanes] = _dot(a_k.astype(BF16), _block_diag(vb))
            kr_ref[rows2, lanes] = kr
            bp_ref[rows, lanes] = (b * to_end).astype(BF16)
            kpv_ref[rows2, lanes] = jnp.where(same_head, _dot_tn((k * to_end).astype(BF16), vb), 0.0)
            dend_ref[rows8, lanes] = jnp.broadcast_to(jnp.exp(ld_end), (SUBLANES, LANES))
        return carry

    lax.fori_loop(0, tb // c, chunk, 0)


def _scan_spec(rows_per_step, tb):
    return pl.BlockSpec((int(tb * rows_per_step), D_MODEL), lambda i: (i, 0))


def _rwkv_scan_operands(r, ld, k, v, kk, b, *, tb):
    n = r.shape[0]
    per_chunk = SUBLANES / SCAN_CHUNK
    one, two, eighth = _scan_spec(1, tb), _scan_spec(2, tb), _scan_spec(per_chunk, tb)
    shape = lambda rows, dt: jax.ShapeDtypeStruct((int(n * rows), D_MODEL), dt)
    return pl.pallas_call(
        functools.partial(_scan_operands_kernel, tb=tb),
        grid=(n // tb,),
        in_specs=[one] * 6,
        out_specs=[one, two, two, one, one, two, eighth],
        out_shape=[shape(1, F32), shape(2, F32), shape(2, BF16), shape(1, BF16), shape(1, BF16), shape(2, F32),
                   shape(per_chunk, F32)],
        compiler_params=pltpu.CompilerParams(dimension_semantics=("parallel",)),
        name="rwkv_scan_operands",
    )(r, ld, k, v, kk, b)


def _scan_kernel(ab_ref, av_ref, kr_ref, arb_ref, bp_ref, kpv_ref, dend_ref, y_ref, st_ref, u_ref, *, tb):
    c = SCAN_CHUNK
    n_blk = c // SUBLANES

    @pl.when(pl.program_id(0) == 0)
    def _():
        st_ref[...] = jnp.zeros_like(st_ref)

    lane8 = lax.broadcasted_iota(jnp.int32, (SUBLANES, LANES), 1)
    rr = lax.broadcasted_iota(jnp.int32, (LANES, LANES), 0)
    cc = lax.broadcasted_iota(jnp.int32, (LANES, LANES), 1)
    same_head = _same_head_mask()
    eye = rr == cc

    def chunk(ci, carry):
        row0 = pl.multiple_of(ci * c, c)
        rows = pl.ds(row0, c)
        row2 = pl.multiple_of(ci * 2 * c, 2 * c)
        rows8 = pl.ds(pl.multiple_of(ci * SUBLANES, SUBLANES), SUBLANES)
        y_state = []
        for p in range(N_PAIRS):
            lanes = slice(p * LANES, (p + 1) * LANES)
            sk = _dot(kr_ref[pl.ds(row2, 2 * c), lanes], st_ref[p].astype(BF16))
            u_ref[p] = sk[:c] + av_ref[pl.ds(row2, c), lanes]
            y_state.append(sk[c:])

        for g in range(n_blk):
            blk = slice(g * SUBLANES, (g + 1) * SUBLANES)
            cur = [u_ref[p, blk, :] for p in range(N_PAIRS)]
            for j in range(SUBLANES):
                i = g * SUBLANES + j
                idx = jnp.where(lane8 >= RW_HEAD, RW_HEAD + i, i)
                for p in range(N_PAIRS):
                    lanes = slice(p * LANES, (p + 1) * LANES)
                    col = jnp.take_along_axis(ab_ref[pl.ds(row0 + g * SUBLANES, SUBLANES), lanes], idx, axis=1)
                    cur[p] = cur[p] - col * cur[p][j:j + 1, :]
            for p in range(N_PAIRS):
                u_ref[p, blk, :] = cur[p]
            n_rest = c - (g + 1) * SUBLANES
            if n_rest:
                for p in range(N_PAIRS):
                    lanes = slice(p * LANES, (p + 1) * LANES)
                    ub = cur[p].astype(BF16)
                    pieces = []
                    for h in range(2):
                        keep = (lane8 >= RW_HEAD) if h else (lane8 < RW_HEAD)
                        if g:
                            pieces.append(jnp.zeros((g * SUBLANES, LANES), BF16))
                        pieces.append(jnp.where(keep, ub, jnp.zeros_like(ub)))
                        pieces.append(jnp.zeros((n_rest, LANES), BF16))
                    below = ab_ref[pl.ds(row0 + (g + 1) * SUBLANES, n_rest), lanes].astype(BF16)
                    rest = slice((g + 1) * SUBLANES, c)
                    u_ref[p, rest, :] = u_ref[p, rest, :] - _dot(below, jnp.concatenate(pieces, axis=0))

        for p in range(N_PAIRS):
            lanes = slice(p * LANES, (p + 1) * LANES)
            u = u_ref[p].astype(BF16)
            y_ref[rows, lanes] = (y_state[p] + av_ref[pl.ds(row2 + c, c), lanes]
                                  - _dot(arb_ref[rows, lanes], _block_diag(u)))
            upd = kpv_ref[pl.ds(row2, 2 * c), lanes] - jnp.where(same_head, _dot_tn(bp_ref[rows, lanes], u), 0.0)
            d_end = dend_ref[rows8, lanes][0:1, :]
            d_col = jnp.sum(jnp.where(eye, d_end, 0.0), axis=1, keepdims=True)
            st_ref[p] = st_ref[p] * d_col + upd
        return carry

    lax.fori_loop(0, tb // c, chunk, 0)


def _rwkv_scan(ab, av, kr, arb, bp, kpv, dend, *, tb):
    n = ab.shape[0]
    one, two, eighth = _scan_spec(1, tb), _scan_spec(2, tb), _scan_spec(SUBLANES / SCAN_CHUNK, tb)
    return pl.pallas_call(
        functools.partial(_scan_kernel, tb=tb),
        grid=(n // tb,),
        in_specs=[one, two, two, one, one, two, eighth],
        out_specs=one,
        out_shape=jax.ShapeDtypeStruct((n, D_MODEL), F32),
        scratch_shapes=[pltpu.VMEM((N_PAIRS, LANES, LANES), F32),
                        pltpu.VMEM((N_PAIRS, SCAN_CHUNK, LANES), F32)],
        compiler_params=pltpu.CompilerParams(dimension_semantics=("arbitrary",)),
        name="rwkv_scan",
    )(ab, av, kr, arb, bp, kpv, dend)


def _rwkv_post_kernel(y_ref, r_ref, k_ref, v_ref, g_ref, x_ref, rk_ref, lg_ref, lb_ref, wo_ref,
                      ng_ref, nb_ref, o_ref):
    ones = _head_ones()
    y = y_ref[...]
    inv_n = 1.0 / RW_HEAD
    c = y - _head_sum(y, ones) * inv_n
    var = _head_sum(c * c, ones) * inv_n
    yn = c * lax.rsqrt(var + GN_EPS) * lg_ref[...] + lb_ref[...]
    bonus = _head_sum(r_ref[...] * k_ref[...] * rk_ref[...], ones) * v_ref[...]
    out = _dot(((yn + bonus) * g_ref[...]).astype(BF16), wo_ref[...])
    o_ref[...] = _layer_norm(DN_ALPHA * x_ref[...] + out, ng_ref[...], nb_ref[...])


def _rwkv_post(y, r, k, v, g, x, r_k, lnx_g, lnx_b, w_o, ln_g, ln_b, *, tb):
    n = x.shape[0]
    row = pl.BlockSpec((tb, D_MODEL), lambda i: (i, 0))
    full = lambda a: pl.BlockSpec(a.shape, lambda i: (0,) * a.ndim)
    params = (r_k, lnx_g, lnx_b, w_o, ln_g, ln_b)
    return pl.pallas_call(
        _rwkv_post_kernel,
        grid=(n // tb,),
        in_specs=[row] * 6 + [full(p) for p in params],
        out_specs=row,
        out_shape=jax.ShapeDtypeStruct((n, D_MODEL), F32),
        compiler_params=pltpu.CompilerParams(dimension_semantics=("parallel",)),
        name="rwkv_output_ln",
    )(y, r, k, v, g, x, *params)


def _row(a):
    return a.reshape(1, -1)


def _moe_layer(x, layer, w_router, e_bias, w_gate, w_up, w_down, ws_gate, ws_up, ws_down, ln_g, ln_b):
    n = x.shape[0]
    gates_t = _router(x, w_router[layer].T, e_bias[layer].reshape(-1, 1), tb=min(512, n))
    return _moe(x, gates_t.T,
                w_gate[layer].astype(BF16), w_up[layer].astype(BF16), w_down[layer].astype(BF16),
                ws_gate[layer].astype(BF16), ws_up[layer].astype(BF16), ws_down[layer].astype(BF16),
                _row(ln_g[layer]), _row(ln_b[layer]), tb=min(1024, n))


def kernel(x, ln_mix_g, ln_mix_b, ln_ffn_g, ln_ffn_b, ab_w_in, ab_pool_w, ab_pool_scale, ab_w_out, rw_mu, rw_w_r, rw_w_k, rw_w_v, rw_w_w1, rw_w_w2, rw_w0, rw_a1, rw_a2, rw_a0, rw_g1, rw_g2, rw_k_k, rw_k_a, rw_r_k, rw_lnx_g, rw_lnx_b, rw_w_o, moe_w_router, moe_e_bias, moe_w_gate, moe_w_up, moe_w_down, moe_ws_gate, moe_ws_up, moe_ws_down):
    batch, seq, d = x.shape
    assert batch == 1 and d == D_MODEL and seq % SCAN_CHUNK == 0
    n = seq
    h = x.reshape(n, d)
    moe_args = (moe_w_router, moe_e_bias, moe_w_gate, moe_w_up, moe_w_down, moe_ws_gate, moe_ws_up, moe_ws_down)

    u, qkv = _inproj(h, ab_w_in[0].astype(BF16), tm=min(512, n))
    y_sb = _stick_breaking(qkv, t=min(128, n))
    h = _mix0(u, y_sb, h, ab_pool_w[0].astype(BF16), _row(ab_pool_scale[0]), ab_w_out[0].astype(BF16),
              _row(ln_mix_g[0]), _row(ln_mix_b[0]), tb=min(256, n))
    h = _moe_layer(h, 0, *moe_args, ln_ffn_g, ln_ffn_b)

    bf = lambda a: a[0].astype(BF16)
    r, ld, k, v, kk, b, g = _rwkv_pre(
        h, rw_mu[0], bf(rw_w_r), bf(rw_w_k), bf(rw_w_v), bf(rw_w_w1), bf(rw_w_w2), _row(rw_w0[0]),
        bf(rw_a1), bf(rw_a2), _row(rw_a0[0]), bf(rw_g1), bf(rw_g2), _row(rw_k_k[0]), _row(rw_k_a[0]),
        tb=min(256, n))
    y = _rwkv_scan(*_rwkv_scan_operands(r, ld, k, v, kk, b, tb=min(256, n)), tb=min(256, n))
    h = _rwkv_post(y, r, k, v, g, h, _row(rw_r_k[0]), _row(rw_lnx_g[0]), _row(rw_lnx_b[0]), bf(rw_w_o),
                   _row(ln_mix_g[1]), _row(ln_mix_b[1]), tb=min(256, n))
    h = _moe_layer(h, 1, *moe_args, ln_ffn_g, ln_ffn_b)
    return h.reshape(batch, seq, d)
```

```python
import functools

import jax
import jax.numpy as jnp
from jax import lax
from jax.experimental import pallas as pl
from jax.experimental.pallas import tpu as pltpu

F32 = jnp.float32
BF16 = jnp.bfloat16

D_MODEL = 1024
DEPTH = 2
POOL_WINDOWS = (2, 4, 8, 16)
POOL_WIDTH = 512
POOL_GROUP = 128
POOL_HALO = 16
SB_HEAD_DIM = 64
SB_WIDTH = 512
RW_HEAD = 64
GN_EPS = 64e-5
N_EXPERTS = 64
TOP_K = 8
N_GROUPS = 8
GROUP_SIZE = N_EXPERTS // N_GROUPS
TOPK_GROUPS = 4
EXPERT_FF = 256
ROUTED_SCALE = 2.5
DN_ALPHA = (2.0 * DEPTH) ** 0.25
LN_EPS = 1e-5

LANES = 128
SUBLANES = 8
EXP_ZERO_BOUND = 110.0

_HIGHEST = lax.Precision.HIGHEST


def _dot(a, b, **kw):
    return jnp.dot(a, b, preferred_element_type=F32, **kw)


def _dot_nt(a, b, **kw):
    return lax.dot_general(a, b, (((1,), (1,)), ((), ())), preferred_element_type=F32, **kw)


def _split_bf16(x):
    hi = x.astype(BF16)
    lo = (x - hi.astype(F32)).astype(BF16)
    return hi, lo


def _layer_norm(h, g, b):
    mu = jnp.mean(h, axis=-1, keepdims=True)
    c = h - mu
    var = jnp.mean(c * c, axis=-1, keepdims=True)
    return c * lax.rsqrt(var + LN_EPS) * g + b


def _sigmoid(x):
    return 1.0 / (1.0 + jnp.exp(-x))


def _softplus(x):
    return jnp.maximum(x, 0.0) + jnp.log1p(jnp.exp(-jnp.abs(x)))


def _head_ones():
    r = lax.broadcasted_iota(jnp.int32, (LANES, LANES), 0)
    c = lax.broadcasted_iota(jnp.int32, (LANES, LANES), 1)
    return ((r // RW_HEAD) == (c // RW_HEAD)).astype(BF16)


def _head_sum(x, ones):
    hi, lo = _split_bf16(x)
    outs = []
    for j in range(x.shape[1] // LANES):
        sl = slice(j * LANES, (j + 1) * LANES)
        outs.append(_dot(hi[:, sl], ones) + _dot(lo[:, sl], ones))
    return jnp.concatenate(outs, axis=1)


def _inproj_kernel(x_ref, w_ref, u_ref, qkv_ref):
    xb = x_ref[...].astype(BF16)
    u_ref[...] = _dot(xb, w_ref[:, :POOL_WIDTH])
    qkv_ref[...] = _dot(xb, w_ref[:, POOL_WIDTH:]).astype(BF16)


def _inproj(x, w_in, *, tm):
    n = x.shape[0]
    return pl.pallas_call(
        _inproj_kernel,
        grid=(n // tm,),
        in_specs=[pl.BlockSpec((tm, D_MODEL), lambda i: (i, 0)),
                  pl.BlockSpec(w_in.shape, lambda i: (0, 0))],
        out_specs=[pl.BlockSpec((tm, POOL_WIDTH), lambda i: (i, 0)),
                   pl.BlockSpec((tm, 3 * SB_WIDTH), lambda i: (i, 0))],
        out_shape=[jax.ShapeDtypeStruct((n, POOL_WIDTH), F32),
                   jax.ShapeDtypeStruct((n, 3 * SB_WIDTH), BF16)],
        compiler_params=pltpu.CompilerParams(dimension_semantics=("parallel",)),
        name="inproj",
    )(x, w_in)


SB_TILE = 128


def _sb_kernel(q_ref, k_ref, v_ref, o_ref, carry_ref, acc_ref):
    t = SB_TILE
    n_sub = q_ref.shape[0] // t
    first = pl.program_id(1) * n_sub
    lane = lax.broadcasted_iota(jnp.int32, (t, LANES), 1)
    row = lax.broadcasted_iota(jnp.int32, (t, t), 0)
    col = lax.broadcasted_iota(jnp.int32, (t, t), 1)
    sums = jnp.concatenate([(row > col).astype(BF16), jnp.ones((t, t), BF16)], axis=1)
    scale = SB_HEAD_DIM ** -0.5
    neg_inf = jnp.float32(-jnp.inf)
    heads = (lane < SB_HEAD_DIM, lane >= SB_HEAD_DIM)

    carry_ref[...] = jnp.zeros_like(carry_ref)
    acc_ref[...] = jnp.zeros_like(acc_ref)

    def body(st):
        m, _ = st
        walks = [(s, h) for s in range(n_sub) for h in range(2)]
        j = [first + s - m for s in range(n_sub)]
        jc = [jnp.maximum(x, 0) for x in j]
        k0 = [pl.multiple_of(x * t, t) for x in jc]
        past = [col + (jc[s] - first - s) * t < row for s in range(n_sub)]
        z = [_dot_nt(jnp.where(heads[h], q_ref[s * t:(s + 1) * t, :], 0), k_ref[pl.ds(k0[s], t), :]) * scale
             for s, h in walks]
        sp = [_softplus(-x) for x in z]
        log_fail = [jnp.where(past[s], -sp[c] - z[c], 0.0) for c, (s, h) in enumerate(walks)]
        split = [_split_bf16(x) for x in log_fail]
        part = [_dot(hi, sums) + _dot(lo, sums) for hi, lo in split]
        w = [jnp.where(past[s], jnp.exp(part[c][:, :t] + carry_ref[c] - sp[c]), 0.0)
             for c, (s, h) in enumerate(walks)]
        pv = [_dot(w[c].astype(BF16), v_ref[pl.ds(k0[s], t), :]) for c, (s, h) in enumerate(walks)]
        live = jnp.full((t, t), neg_inf, F32)
        for c, (s, h) in enumerate(walks):
            acc_ref[c] += jnp.where(j[s] >= 0, pv[c], 0.0)
            carry = carry_ref[c] + part[c][:, t:]
            carry_ref[c] = carry
            live = jnp.maximum(live, jnp.where(j[s] >= 1, carry, neg_inf))
        return m + 1, jnp.max(live)

    lax.while_loop(lambda st: st[1] > -EXP_ZERO_BOUND, body, (jnp.int32(0), jnp.float32(0.0)))
    for s in range(n_sub):
        o_ref[s * t:(s + 1) * t, :] = jnp.where(heads[0], acc_ref[2 * s], acc_ref[2 * s + 1]).astype(BF16)


def _stick_breaking(qkv, *, tq):
    n = qkv.shape[0]
    n_pairs = SB_WIDTH // LANES
    n_walks = 2 * tq // SB_TILE
    return pl.pallas_call(
        _sb_kernel,
        grid=(n_pairs, n // tq),
        in_specs=[pl.BlockSpec((tq, LANES), lambda p, i: (i, p)),
                  pl.BlockSpec((n, LANES), lambda p, i: (0, n_pairs + p)),
                  pl.BlockSpec((n, LANES), lambda p, i: (0, 2 * n_pairs + p))],
        out_specs=pl.BlockSpec((tq, LANES), lambda p, i: (i, p)),
        out_shape=jax.ShapeDtypeStruct((n, SB_WIDTH), BF16),
        scratch_shapes=[pltpu.VMEM((n_walks, SB_TILE, SB_TILE), F32),
                        pltpu.VMEM((n_walks, SB_TILE, LANES), F32)],
        compiler_params=pltpu.CompilerParams(dimension_semantics=("parallel", "parallel")),
        name="stick_breaking",
    )(qkv, qkv, qkv)


def _mix0_kernel(u_ref, halo_ref, ysb_ref, x_ref, pw_ref, ps_ref, wo_ref, g_ref, b_ref, o_ref, ext_ref, *, tb):
    i = pl.program_id(0)
    u = u_ref[...]
    ext_ref[0:POOL_HALO, :] = jnp.where(i > 0, halo_ref[...], 0.0)
    ext_ref[POOL_HALO:, :] = u
    t_glob = i * tb + lax.broadcasted_iota(jnp.int32, (tb, 1), 0)
    parts = []
    for g, win in enumerate(POOL_WINDOWS):
        cols = slice(g * POOL_GROUP, (g + 1) * POOL_GROUP)
        s = u[:, cols]
        for j in range(1, win):
            s = s + ext_ref[POOL_HALO - j:POOL_HALO - j + tb, cols]
        cnt = jnp.minimum(t_glob + 1, win).astype(F32)
        diff = s / cnt - u[:, cols]
        parts.append(_dot(diff.astype(BF16), pw_ref[g]))
    y_pool = jnp.concatenate(parts, axis=1) * ps_ref[...]
    mix = _dot(y_pool.astype(BF16), wo_ref[:POOL_WIDTH, :]) + _dot(ysb_ref[...], wo_ref[POOL_WIDTH:, :])
    o_ref[...] = _layer_norm(DN_ALPHA * x_ref[...] + mix, g_ref[...], b_ref[...])


def _mix0(u, y_sb, x, pool_w, pool_scale, w_out, ln_g, ln_b, *, tb):
    n = x.shape[0]
    halo_blocks = tb // POOL_HALO
    row = lambda i: (i, 0)
    full = lambda a: pl.BlockSpec(a.shape, lambda i: (0,) * a.ndim)
    return pl.pallas_call(
        functools.partial(_mix0_kernel, tb=tb),
        grid=(n // tb,),
        in_specs=[pl.BlockSpec((tb, POOL_WIDTH), row),
                  pl.BlockSpec((POOL_HALO, POOL_WIDTH), lambda i: (jnp.maximum(i * halo_blocks - 1, 0), 0)),
                  pl.BlockSpec((tb, SB_WIDTH), row),
                  pl.BlockSpec((tb, D_MODEL), row),
                  full(pool_w), full(pool_scale), full(w_out), full(ln_g), full(ln_b)],
        out_specs=pl.BlockSpec((tb, D_MODEL), row),
        out_shape=jax.ShapeDtypeStruct((n, D_MODEL), F32),
        scratch_shapes=[pltpu.VMEM((tb + POOL_HALO, POOL_WIDTH), F32)],
        compiler_params=pltpu.CompilerParams(dimension_semantics=("parallel",)),
        name="pool_outproj_ln",
    )(u, u, y_sb, x, pool_w, pool_scale, w_out, ln_g, ln_b)


def _first_argmax_hit(v, idx, size):
    m = jnp.max(v, axis=0, keepdims=True)
    first = jnp.min(jnp.where(v == m, idx, size), axis=0, keepdims=True)
    return idx == first


def _router_kernel(x_ref, wrt_ref, eb_ref, gates_ref):
    tb = x_ref.shape[0]
    logits = _dot_nt(wrt_ref[...], x_ref[...], precision=_HIGHEST)
    scores = _sigmoid(logits)
    biased = scores + eb_ref[...]
    neg_inf = jnp.float32(-jnp.inf)

    sub = lax.broadcasted_iota(jnp.int32, (GROUP_SIZE, tb), 0)
    group_scores = []
    for g in range(N_GROUPS):
        bg = biased[g * GROUP_SIZE:(g + 1) * GROUP_SIZE, :]
        m1 = jnp.max(bg, axis=0, keepdims=True)
        rest = jnp.where(_first_argmax_hit(bg, sub, GROUP_SIZE), neg_inf, bg)
        group_scores.append(m1 + jnp.max(rest, axis=0, keepdims=True))
    gs = jnp.concatenate(group_scores, axis=0)

    g_idx = lax.broadcasted_iota(jnp.int32, (N_GROUPS, tb), 0)
    group_on = jnp.zeros((N_GROUPS, tb), jnp.bool_)
    for _ in range(TOPK_GROUPS):
        hit = _first_argmax_hit(gs, g_idx, N_GROUPS)
        group_on = jnp.logical_or(group_on, hit)
        gs = jnp.where(hit, neg_inf, gs)

    masked = jnp.concatenate(
        [jnp.where(group_on[g:g + 1, :], biased[g * GROUP_SIZE:(g + 1) * GROUP_SIZE, :], neg_inf)
         for g in range(N_GROUPS)], axis=0)
    e_idx = lax.broadcasted_iota(jnp.int32, (N_EXPERTS, tb), 0)
    chosen = jnp.zeros((N_EXPERTS, tb), jnp.bool_)
    for _ in range(TOP_K):
        hit = _first_argmax_hit(masked, e_idx, N_EXPERTS)
        chosen = jnp.logical_or(chosen, hit)
        masked = jnp.where(hit, neg_inf, masked)

    sel = jnp.where(chosen, scores, 0.0)
    gates = sel / jnp.sum(sel, axis=0, keepdims=True) * ROUTED_SCALE
    gates_ref[...] = jnp.concatenate([gates, jnp.zeros((LANES - N_EXPERTS, tb), F32)], axis=0).T


def _router(x, w_router_t, e_bias_col, *, tb):
    n = x.shape[0]
    return pl.pallas_call(
        _router_kernel,
        grid=(n // tb,),
        in_specs=[pl.BlockSpec((tb, D_MODEL), lambda i: (i, 0)),
                  pl.BlockSpec(w_router_t.shape, lambda i: (0, 0)),
                  pl.BlockSpec(e_bias_col.shape, lambda i: (0, 0))],
        out_specs=pl.BlockSpec((tb, LANES), lambda i: (i, 0)),
        out_shape=jax.ShapeDtypeStruct((n, LANES), F32),
        compiler_params=pltpu.CompilerParams(dimension_semantics=("parallel",)),
        name="router",
    )(x, w_router_t, e_bias_col)


def _swiglu_hidden(xb, wg, wu):
    hg = _dot(xb, wg)
    return hg * _sigmoid(hg) * _dot(xb, wu)


MOE_EXPERTS_PER_STEP = 2


def _moe_kernel(x_ref, gates_ref, wg_ref, wu_ref, wd_ref, sg_ref, su_ref, sd_ref, g_ref, b_ref, o_ref,
                xb_ref, acc_ref):
    step = pl.program_id(1)

    @pl.when(step == 0)
    def _():
        xb = x_ref[...].astype(BF16)
        xb_ref[...] = xb
        acc_ref[...] = _dot(_swiglu_hidden(xb, sg_ref[...], su_ref[...]).astype(BF16), sd_ref[...])

    xb = xb_ref[...]
    gates = gates_ref[...]
    group = range(MOE_EXPERTS_PER_STEP)
    hidden = [_swiglu_hidden(xb, wg_ref[j].astype(BF16), wu_ref[j].astype(BF16)) for j in group]
    total = acc_ref[...]
    for j in group:
        e = step * MOE_EXPERTS_PER_STEP + j
        gate = jnp.take_along_axis(gates, jnp.full(gates.shape, e, jnp.int32), axis=1)
        act = hidden[j] * jnp.concatenate([gate] * (EXPERT_FF // LANES), axis=1)
        total = total + _dot(act.astype(BF16), wd_ref[j].astype(BF16))
    acc_ref[...] = total

    @pl.when(step == pl.num_programs(1) - 1)
    def _():
        o_ref[...] = _layer_norm(DN_ALPHA * x_ref[...] + acc_ref[...], g_ref[...], b_ref[...])


def _moe(x, gates, layer, wg, wu, wd, sg, su, sd, ln_g, ln_b, *, tb):
    n = x.shape[0]
    group = MOE_EXPERTS_PER_STEP
    tok = lambda i, s: (i, 0)
    exp = lambda i, s: (layer, s, 0, 0)
    full = lambda a: pl.BlockSpec(a.shape, lambda i, s: (0,) * a.ndim)
    return pl.pallas_call(
        _moe_kernel,
        grid=(n // tb, N_EXPERTS // group),
        in_specs=[pl.BlockSpec((tb, D_MODEL), tok),
                  pl.BlockSpec((tb, LANES), tok),
                  pl.BlockSpec((None, group, D_MODEL, EXPERT_FF), exp),
                  pl.BlockSpec((None, group, D_MODEL, EXPERT_FF), exp),
                  pl.BlockSpec((None, group, EXPERT_FF, D_MODEL), exp),
                  full(sg), full(su), full(sd), full(ln_g), full(ln_b)],
        out_specs=pl.BlockSpec((tb, D_MODEL), tok),
        out_shape=jax.ShapeDtypeStruct((n, D_MODEL), F32),
        scratch_shapes=[pltpu.VMEM((tb, D_MODEL), BF16),
                        pltpu.VMEM((tb, D_MODEL), F32)],
        compiler_params=pltpu.CompilerParams(dimension_semantics=("parallel", "arbitrary")),
        name="moe_experts_ln",
    )(x, gates, wg, wu, wd, sg, su, sd, ln_g, ln_b)


def _rwkv_pre_kernel(x_ref, prev_ref, mu_ref, wr_ref, wk_ref, wv_ref, w1_ref, w2_ref, w0_ref,
                     a1_ref, a2_ref, a0_ref, g1_ref, g2_ref, kk_scale_ref, ka_ref, rk_ref,
                     ab_ref, av_ref, kr_ref, arb_ref, bp_ref, kpv_ref, dend_ref, g_ref, bonus_ref,
                     ext_ref, r_ref, ld_ref, k_ref, v_ref, kk_ref, b_ref, *, tb):
    i = pl.program_id(0)
    x = x_ref[...]
    ext_ref[0:SUBLANES, :] = jnp.where(i > 0, prev_ref[...], 0.0)
    ext_ref[SUBLANES:, :] = x
    xx = ext_ref[SUBLANES - 1:SUBLANES - 1 + tb, :] - x
    mix = lambda j: (x + xx * mu_ref[j:j + 1, :]).astype(BF16)
    xr, xw, xk, xv, xa, xg = [mix(j) for j in range(6)]

    r = _dot(xr, wr_ref[...])
    w = w0_ref[...] + _dot(jnp.tanh(_dot(xw, w1_ref[...])).astype(BF16), w2_ref[...])
    w_log = -_softplus(-w) - 0.5
    ld_ref[...] = -jnp.exp(w_log)
    k = _dot(xk, wk_ref[...])
    v = _dot(xv, wv_ref[...])
    a = _sigmoid(a0_ref[...] + _dot(_dot(xa, a1_ref[...]).astype(BF16), a2_ref[...]))
    g_ref[...] = _dot(_sigmoid(_dot(xg, g1_ref[...])).astype(BF16), g2_ref[...]).astype(BF16)

    ones = _head_ones()
    kk = k * kk_scale_ref[...]
    kk = kk / jnp.maximum(jnp.sqrt(_head_sum(kk * kk, ones)), 1e-12)
    k = k * (1.0 + (a - 1.0) * ka_ref[...])
    bonus_ref[...] = (_head_sum(r * k * rk_ref[...], ones) * v).astype(BF16)
    r_ref[...] = r
    k_ref[...] = k
    v_ref[...] = v
    kk_ref[...] = kk
    b_ref[...] = kk * a
    _scan_operands(r_ref, ld_ref, k_ref, v_ref, kk_ref, b_ref,
                   ab_ref, av_ref, kr_ref, arb_ref, bp_ref, kpv_ref, dend_ref, tb=tb)


def _rwkv_pre(x, mu, wr, wk, wv, w1, w2, w0, a1, a2, a0, g1, g2, k_k, k_a, r_k, *, tb):
    n = x.shape[0]
    row = lambda i: (i, 0)
    full = lambda a: pl.BlockSpec(a.shape, lambda i: (0,) * a.ndim)
    params = (mu, wr, wk, wv, w1, w2, w0, a1, a2, a0, g1, g2, k_k, k_a, r_k)
    per_chunk = SUBLANES / SCAN_CHUNK
    one, two, eighth = _scan_spec(1, tb), _scan_spec(2, tb), _scan_spec(per_chunk, tb)
    shape = lambda rows, dt: jax.ShapeDtypeStruct((int(n * rows), D_MODEL), dt)
    return pl.pallas_call(
        functools.partial(_rwkv_pre_kernel, tb=tb),
        grid=(n // tb,),
        in_specs=[pl.BlockSpec((tb, D_MODEL), row),
                  pl.BlockSpec((SUBLANES, D_MODEL), lambda i: (jnp.maximum(i * (tb // SUBLANES) - 1, 0), 0))]
                 + [full(p) for p in params],
        out_specs=[one, two, two, one, one, two, eighth, one, one],
        out_shape=[shape(1, F32), shape(2, F32), shape(2, BF16), shape(1, BF16), shape(1, BF16), shape(2, F32),
                   shape(per_chunk, F32), shape(1, BF16), shape(1, BF16)],
        scratch_shapes=[pltpu.VMEM((tb + SUBLANES, D_MODEL), F32)] + [pltpu.VMEM((tb, D_MODEL), F32)] * 6,
        compiler_params=pltpu.CompilerParams(dimension_semantics=("parallel",)),
        name="rwkv_projections",
    )(x, x, *params)


SCAN_CHUNK = 64
N_PAIRS = D_MODEL // LANES


def _dot_tn(a, b):
    return lax.dot_general(a, b, (((0,), (0,)), ((), ())), preferred_element_type=F32)


def _split3_bf16(x):
    hi = x.astype(BF16)
    r1 = x - hi.astype(F32)
    mid = r1.astype(BF16)
    return hi, mid, (r1 - mid.astype(F32)).astype(BF16)


def _block_diag(x):
    lane = lax.broadcasted_iota(jnp.int32, x.shape, 1)
    zero = jnp.zeros_like(x)
    return jnp.concatenate([jnp.where(lane < RW_HEAD, x, zero), jnp.where(lane >= RW_HEAD, x, zero)], axis=0)


def _same_head_mask():
    r = lax.broadcasted_iota(jnp.int32, (LANES, LANES), 0)
    c = lax.broadcasted_iota(jnp.int32, (LANES, LANES), 1)
    return (r // RW_HEAD) == (c // RW_HEAD)


def _scan_operands(r_ref, ld_ref, k_ref, v_ref, kk_ref, b_ref,
                   ab_ref, av_ref, kr_ref, arb_ref, bp_ref, kpv_ref, dend_ref, *, tb):
    c = SCAN_CHUNK
    row = lax.broadcasted_iota(jnp.int32, (c, LANES), 0)
    li = lax.broadcasted_iota(jnp.int32, (c, LANES), 1) % RW_HEAD
    strict = li < row
    incl = li <= row
    r2 = lax.broadcasted_iota(jnp.int32, (c, c), 0)
    c2 = lax.broadcasted_iota(jnp.int32, (c, c), 1)
    cumsum = (c2 <= r2).astype(BF16)
    same_head = _same_head_mask()

    def chunk(ci, carry):
        rows = pl.ds(pl.multiple_of(ci * c, c), c)
        rows2 = pl.ds(pl.multiple_of(ci * 2 * c, 2 * c), 2 * c)
        rows8 = pl.ds(pl.multiple_of(ci * SUBLANES, SUBLANES), SUBLANES)
        pairs = range(N_PAIRS)
        lanes = [slice(p * LANES, (p + 1) * LANES) for p in pairs]
        ld_step = [ld_ref[rows, lanes[p]] for p in pairs]
        parts = [_split3_bf16(x) for x in ld_step]
        ld = [_dot(cumsum, hi) + _dot(cumsum, mid) + _dot(cumsum, lo) for hi, mid, lo in parts]
        ld_end = [x[c - 1:c, :] for x in ld]
        inv_d = [jnp.exp(-x) for x in ld]
        vb = [v_ref[rows, lanes[p]].astype(BF16) for p in pairs]
        kr = [jnp.concatenate([kk_ref[rows, lanes[p]] * jnp.exp(ld[p] - ld_step[p]),
                               r_ref[rows, lanes[p]] * jnp.exp(ld[p])], axis=0).astype(BF16) for p in pairs]
        a_k = [_dot_nt(kr[p], _block_diag((k_ref[rows, lanes[p]] * inv_d[p]).astype(BF16))) for p in pairs]
        a_b = [_dot_nt(kr[p], _block_diag((b_ref[rows, lanes[p]] * inv_d[p]).astype(BF16))) for p in pairs]
        for p in pairs:
            ab_ref[rows, lanes[p]] = jnp.where(strict, a_b[p][:c], 0.0)
            arb_ref[rows, lanes[p]] = jnp.where(incl, a_b[p][c:], 0.0).astype(BF16)
            kr_ref[rows2, lanes[p]] = kr[p]
        a_k = [jnp.concatenate([jnp.where(strict, x[:c], 0.0), jnp.where(incl, x[c:], 0.0)], axis=0).astype(BF16)
               for x in a_k]
        av = [_dot(a_k[p], _block_diag(vb[p])) for p in pairs]
        to_end = [jnp.exp(ld_end[p] - ld[p]) for p in pairs]
        kpv = [_dot_tn((k_ref[rows, lanes[p]] * to_end[p]).astype(BF16), vb[p]) for p in pairs]
        for p in pairs:
            av_ref[rows2, lanes[p]] = av[p]
            bp_ref[rows, lanes[p]] = (b_ref[rows, lanes[p]] * to_end[p]).astype(BF16)
            kpv_ref[rows2, lanes[p]] = jnp.where(same_head, kpv[p], 0.0)
            dend_ref[rows8, lanes[p]] = jnp.broadcast_to(jnp.exp(ld_end[p]), (SUBLANES, LANES))
        return carry

    lax.fori_loop(0, tb // c, chunk, 0)


def _scan_spec(rows_per_step, tb):
    return pl.BlockSpec((int(tb * rows_per_step), D_MODEL), lambda i: (i, 0))


def _scan_kernel(ab_ref, av_ref, kr_ref, arb_ref, bp_ref, kpv_ref, dend_ref, y_ref, st_ref, u_ref, *, tb):
    c = SCAN_CHUNK
    n_blk = c // SUBLANES

    @pl.when(pl.program_id(0) == 0)
    def _():
        st_ref[...] = jnp.zeros_like(st_ref)

    lane8 = lax.broadcasted_iota(jnp.int32, (SUBLANES, LANES), 1)
    rr = lax.broadcasted_iota(jnp.int32, (LANES, LANES), 0)
    cc = lax.broadcasted_iota(jnp.int32, (LANES, LANES), 1)
    same_head = _same_head_mask()
    eye = rr == cc

    def chunk(ci, carry):
        row0 = pl.multiple_of(ci * c, c)
        rows = pl.ds(row0, c)
        row2 = pl.multiple_of(ci * 2 * c, 2 * c)
        rows8 = pl.ds(pl.multiple_of(ci * SUBLANES, SUBLANES), SUBLANES)
        y_state = []
        for p in range(N_PAIRS):
            lanes = slice(p * LANES, (p + 1) * LANES)
            sk = _dot(kr_ref[pl.ds(row2, 2 * c), lanes], st_ref[p].astype(BF16))
            u_ref[p] = sk[:c] + av_ref[pl.ds(row2, c), lanes]
            y_state.append(sk[c:])

        for g in range(n_blk):
            blk = slice(g * SUBLANES, (g + 1) * SUBLANES)
            cur = [u_ref[p, blk, :] for p in range(N_PAIRS)]
            for j in range(SUBLANES):
                i = g * SUBLANES + j
                idx = jnp.where(lane8 >= RW_HEAD, RW_HEAD + i, i)
                for p in range(N_PAIRS):
                    lanes = slice(p * LANES, (p + 1) * LANES)
                    col = jnp.take_along_axis(ab_ref[pl.ds(row0 + g * SUBLANES, SUBLANES), lanes], idx, axis=1)
                    cur[p] = cur[p] - col * cur[p][j:j + 1, :]
            for p in range(N_PAIRS):
                u_ref[p, blk, :] = cur[p]
            n_rest = c - (g + 1) * SUBLANES
            if n_rest:
                for p in range(N_PAIRS):
                    lanes = slice(p * LANES, (p + 1) * LANES)
                    ub = cur[p].astype(BF16)
                    pieces = []
                    for h in range(2):
                        keep = (lane8 >= RW_HEAD) if h else (lane8 < RW_HEAD)
                        if g:
                            pieces.append(jnp.zeros((g * SUBLANES, LANES), BF16))
                        pieces.append(jnp.where(keep, ub, jnp.zeros_like(ub)))
                        pieces.append(jnp.zeros((n_rest, LANES), BF16))
                    below = ab_ref[pl.ds(row0 + (g + 1) * SUBLANES, n_rest), lanes].astype(BF16)
                    rest = slice((g + 1) * SUBLANES, c)
                    u_ref[p, rest, :] = u_ref[p, rest, :] - _dot(below, jnp.concatenate(pieces, axis=0))

        for p in range(N_PAIRS):
            lanes = slice(p * LANES, (p + 1) * LANES)
            u = u_ref[p].astype(BF16)
            y_ref[rows, lanes] = (y_state[p] + av_ref[pl.ds(row2 + c, c), lanes]
                                  - _dot(arb_ref[rows, lanes], _block_diag(u)))
            upd = kpv_ref[pl.ds(row2, 2 * c), lanes] - jnp.where(same_head, _dot_tn(bp_ref[rows, lanes], u), 0.0)
            d_end = dend_ref[rows8, lanes][0:1, :]
            d_col = jnp.sum(jnp.where(eye, d_end, 0.0), axis=1, keepdims=True)
            st_ref[p] = st_ref[p] * d_col + upd
        return carry

    lax.fori_loop(0, tb // c, chunk, 0)


def _rwkv_scan(ab, av, kr, arb, bp, kpv, dend, *, tb):
    n = ab.shape[0]
    one, two, eighth = _scan_spec(1, tb), _scan_spec(2, tb), _scan_spec(SUBLANES / SCAN_CHUNK, tb)
    return pl.pallas_call(
        functools.partial(_scan_kernel, tb=tb),
        grid=(n // tb,),
        in_specs=[one, two, two, one, one, two, eighth],
        out_specs=one,
        out_shape=jax.ShapeDtypeStruct((n, D_MODEL), F32),
        scratch_shapes=[pltpu.VMEM((N_PAIRS, LANES, LANES), F32),
                        pltpu.VMEM((N_PAIRS, SCAN_CHUNK, LANES), F32)],
        compiler_params=pltpu.CompilerParams(dimension_semantics=("arbitrary",)),
        name="rwkv_scan",
    )(ab, av, kr, arb, bp, kpv, dend)


def _rwkv_post_kernel(y_ref, g_ref, bonus_ref, x_ref, lg_ref, lb_ref, wo_ref, ng_ref, nb_ref, o_ref):
    ones = _head_ones()
    y = y_ref[...]
    inv_n = 1.0 / RW_HEAD
    c = y - _head_sum(y, ones) * inv_n
    var = _head_sum(c * c, ones) * inv_n
    yn = c * lax.rsqrt(var + GN_EPS) * lg_ref[...] + lb_ref[...]
    out = _dot(((yn + bonus_ref[...].astype(F32)) * g_ref[...].astype(F32)).astype(BF16), wo_ref[...])
    o_ref[...] = _layer_norm(DN_ALPHA * x_ref[...] + out, ng_ref[...], nb_ref[...])


def _rwkv_post(y, g, bonus, x, lnx_g, lnx_b, w_o, ln_g, ln_b, *, tb):
    n = x.shape[0]
    row = pl.BlockSpec((tb, D_MODEL), lambda i: (i, 0))
    full = lambda a: pl.BlockSpec(a.shape, lambda i: (0,) * a.ndim)
    params = (lnx_g, lnx_b, w_o, ln_g, ln_b)
    return pl.pallas_call(
        _rwkv_post_kernel,
        grid=(n // tb,),
        in_specs=[row] * 4 + [full(p) for p in params],
        out_specs=row,
        out_shape=jax.ShapeDtypeStruct((n, D_MODEL), F32),
        compiler_params=pltpu.CompilerParams(dimension_semantics=("parallel",)),
        name="rwkv_output_ln",
    )(y, g, bonus, x, *params)


def _row(a):
    return a.reshape(1, -1)


def _moe_layer(x, layer, w_router, e_bias, w_gate, w_up, w_down, ws_gate, ws_up, ws_down, ln_g, ln_b):
    n = x.shape[0]
    gates = _router(x, w_router[layer].T, e_bias[layer].reshape(-1, 1), tb=min(512, n))
    return _moe(x, gates, layer, w_gate, w_up, w_down,
                ws_gate[layer].astype(BF16), ws_up[layer].astype(BF16), ws_down[layer].astype(BF16),
                _row(ln_g[layer]), _row(ln_b[layer]), tb=min(1024, n))


def kernel(x, ln_mix_g, ln_mix_b, ln_ffn_g, ln_ffn_b, ab_w_in, ab_pool_w, ab_pool_scale, ab_w_out, rw_mu, rw_w_r, rw_w_k, rw_w_v, rw_w_w1, rw_w_w2, rw_w0, rw_a1, rw_a2, rw_a0, rw_g1, rw_g2, rw_k_k, rw_k_a, rw_r_k, rw_lnx_g, rw_lnx_b, rw_w_o, moe_w_router, moe_e_bias, moe_w_gate, moe_w_up, moe_w_down, moe_ws_gate, moe_ws_up, moe_ws_down):
    batch, seq, d = x.shape
    assert batch == 1 and d == D_MODEL and seq % SCAN_CHUNK == 0
    n = seq
    h = x.reshape(n, d)
    moe_args = (moe_w_router, moe_e_bias, moe_w_gate, moe_w_up, moe_w_down, moe_ws_gate, moe_ws_up, moe_ws_down)

    u, qkv = _inproj(h, ab_w_in[0].astype(BF16), tm=min(512, n))
    y_sb = _stick_breaking(qkv, tq=min(512, n))
    h = _mix0(u, y_sb, h, ab_pool_w[0].astype(BF16), _row(ab_pool_scale[0]), ab_w_out[0].astype(BF16),
              _row(ln_mix_g[0]), _row(ln_mix_b[0]), tb=min(256, n))
    h = _moe_layer(h, 0, *moe_args, ln_ffn_g, ln_ffn_b)

    bf = lambda a: a[0].astype(BF16)
    *scan_operands, g, bonus = _rwkv_pre(
        h, rw_mu[0], bf(rw_w_r), bf(rw_w_k), bf(rw_w_v), bf(rw_w_w1), bf(rw_w_w2), _row(rw_w0[0]),
        bf(rw_a1), bf(rw_a2), _row(rw_a0[0]), bf(rw_g1), bf(rw_g2), _row(rw_k_k[0]), _row(rw_k_a[0]),
        _row(rw_r_k[0]), tb=min(256, n))
    y = _rwkv_scan(*scan_operands, tb=min(256, n))
    h = _rwkv_post(y, g, bonus, h, _row(rw_lnx_g[0]), _row(rw_lnx_b[0]), bf(rw_w_o),
                   _row(ln_mix_g[1]), _row(ln_mix_b[1]), tb=min(256, n))
    h = _moe_layer(h, 1, *moe_args, ln_ffn_g, ln_ffn_b)
    return h.reshape(batch, seq, d)
```

```python
import functools

import jax
import jax.numpy as jnp
from jax import lax
from jax.experimental import pallas as pl
from jax.experimental.pallas import tpu as pltpu

F32 = jnp.float32
BF16 = jnp.bfloat16

D_MODEL = 1024
DEPTH = 2
POOL_WINDOWS = (2, 4, 8, 16)
POOL_WIDTH = 512
POOL_GROUP = 128
POOL_HALO = 16
SB_HEAD_DIM = 64
SB_WIDTH = 512
RW_HEAD = 64
GN_EPS = 64e-5
N_EXPERTS = 64
TOP_K = 8
N_GROUPS = 8
GROUP_SIZE = N_EXPERTS // N_GROUPS
TOPK_GROUPS = 4
EXPERT_FF = 256
ROUTED_SCALE = 2.5
DN_ALPHA = (2.0 * DEPTH) ** 0.25
LN_EPS = 1e-5

LANES = 128
SUBLANES = 8
EXP_ZERO_BOUND = 110.0

_HIGHEST = lax.Precision.HIGHEST


def _dot(a, b, **kw):
    return jnp.dot(a, b, preferred_element_type=F32, **kw)


def _dot_nt(a, b, **kw):
    return lax.dot_general(a, b, (((1,), (1,)), ((), ())), preferred_element_type=F32, **kw)


def _split_bf16(x):
    hi = x.astype(BF16)
    lo = (x - hi.astype(F32)).astype(BF16)
    return hi, lo


def _layer_norm(h, g, b):
    mu = jnp.mean(h, axis=-1, keepdims=True)
    c = h - mu
    var = jnp.mean(c * c, axis=-1, keepdims=True)
    return c * lax.rsqrt(var + LN_EPS) * g + b


def _sigmoid(x):
    return 1.0 / (1.0 + jnp.exp(-x))


def _softplus(x):
    return jnp.maximum(x, 0.0) + jnp.log(1.0 + jnp.exp(-jnp.abs(x)))


def _head_ones():
    r = lax.broadcasted_iota(jnp.int32, (LANES, LANES), 0)
    c = lax.broadcasted_iota(jnp.int32, (LANES, LANES), 1)
    return ((r // RW_HEAD) == (c // RW_HEAD)).astype(BF16)


def _head_sum(x, ones):
    hi, lo = _split_bf16(x)
    outs = []
    for j in range(x.shape[1] // LANES):
        sl = slice(j * LANES, (j + 1) * LANES)
        outs.append(_dot(hi[:, sl], ones) + _dot(lo[:, sl], ones))
    return jnp.concatenate(outs, axis=1)


def _inproj_kernel(x_ref, w_ref, u_ref, qkv_ref):
    xb = x_ref[...].astype(BF16)
    u_ref[...] = _dot(xb, w_ref[:, :POOL_WIDTH])
    qkv_ref[...] = _dot(xb, w_ref[:, POOL_WIDTH:]).astype(BF16)


def _inproj(x, w_in, *, tm):
    n = x.shape[0]
    return pl.pallas_call(
        _inproj_kernel,
        grid=(n // tm,),
        in_specs=[pl.BlockSpec((tm, D_MODEL), lambda i: (i, 0)),
                  pl.BlockSpec(w_in.shape, lambda i: (0, 0))],
        out_specs=[pl.BlockSpec((tm, POOL_WIDTH), lambda i: (i, 0)),
                   pl.BlockSpec((tm, 3 * SB_WIDTH), lambda i: (i, 0))],
        out_shape=[jax.ShapeDtypeStruct((n, POOL_WIDTH), F32),
                   jax.ShapeDtypeStruct((n, 3 * SB_WIDTH), BF16)],
        compiler_params=pltpu.CompilerParams(dimension_semantics=("parallel",)),
        name="inproj",
    )(x, w_in)


SB_TILE = 128


def _sb_kernel(q_ref, k_ref, v_ref, o_ref, carry_ref, acc_ref):
    t = SB_TILE
    n_sub = q_ref.shape[0] // t
    first = pl.program_id(1) * n_sub
    lane = lax.broadcasted_iota(jnp.int32, (t, LANES), 1)
    row = lax.broadcasted_iota(jnp.int32, (t, t), 0)
    col = lax.broadcasted_iota(jnp.int32, (t, t), 1)
    sums = jnp.concatenate([(row > col).astype(BF16), jnp.ones((t, t), BF16)], axis=1)
    scale = SB_HEAD_DIM ** -0.5
    neg_inf = jnp.float32(-jnp.inf)
    heads = (lane < SB_HEAD_DIM, lane >= SB_HEAD_DIM)

    carry_ref[...] = jnp.zeros_like(carry_ref)
    acc_ref[...] = jnp.zeros_like(acc_ref)

    def body(st):
        m, _ = st
        walks = [(s, h) for s in range(n_sub) for h in range(2)]
        j = [first + s - m for s in range(n_sub)]
        jc = [jnp.maximum(x, 0) for x in j]
        k0 = [pl.multiple_of(x * t, t) for x in jc]
        past = [col + (jc[s] - first - s) * t < row for s in range(n_sub)]
        z = [_dot_nt(jnp.where(heads[h], q_ref[s * t:(s + 1) * t, :], 0), k_ref[pl.ds(k0[s], t), :]) * scale
             for s, h in walks]
        sp = [_softplus(-x) for x in z]
        log_fail = [jnp.where(past[s], -sp[c] - z[c], 0.0) for c, (s, h) in enumerate(walks)]
        split = [_split_bf16(x) for x in log_fail]
        part = [_dot(hi, sums) + _dot(lo, sums) for hi, lo in split]
        w = [jnp.where(past[s], jnp.exp(part[c][:, :t] + carry_ref[c] - sp[c]), 0.0)
             for c, (s, h) in enumerate(walks)]
        pv = [_dot(w[c].astype(BF16), v_ref[pl.ds(k0[s], t), :]) for c, (s, h) in enumerate(walks)]
        live = jnp.full((t, t), neg_inf, F32)
        for c, (s, h) in enumerate(walks):
            acc_ref[c] += jnp.where(j[s] >= 0, pv[c], 0.0)
            carry = carry_ref[c] + part[c][:, t:]
            carry_ref[c] = carry
            live = jnp.maximum(live, jnp.where(j[s] >= 1, carry, neg_inf))
        return m + 1, jnp.max(live)

    lax.while_loop(lambda st: st[1] > -EXP_ZERO_BOUND, body, (jnp.int32(0), jnp.float32(0.0)))
    for s in range(n_sub):
        o_ref[s * t:(s + 1) * t, :] = jnp.where(heads[0], acc_ref[2 * s], acc_ref[2 * s + 1]).astype(BF16)


def _stick_breaking(qkv, *, tq):
    n = qkv.shape[0]
    n_pairs = SB_WIDTH // LANES
    n_walks = 2 * tq // SB_TILE
    return pl.pallas_call(
        _sb_kernel,
        grid=(n_pairs, n // tq),
        in_specs=[pl.BlockSpec((tq, LANES), lambda p, i: (i, p)),
                  pl.BlockSpec((n, LANES), lambda p, i: (0, n_pairs + p)),
                  pl.BlockSpec((n, LANES), lambda p, i: (0, 2 * n_pairs + p))],
        out_specs=pl.BlockSpec((tq, LANES), lambda p, i: (i, p)),
        out_shape=jax.ShapeDtypeStruct((n, SB_WIDTH), BF16),
        scratch_shapes=[pltpu.VMEM((n_walks, SB_TILE, SB_TILE), F32),
                        pltpu.VMEM((n_walks, SB_TILE, LANES), F32)],
        compiler_params=pltpu.CompilerParams(dimension_semantics=("parallel", "parallel")),
        name="stick_breaking",
    )(qkv, qkv, qkv)


def _mix0_kernel(u_ref, halo_ref, ysb_ref, x_ref, pw_ref, ps_ref, wo_ref, g_ref, b_ref, o_ref, ext_ref, *, tb):
    i = pl.program_id(0)
    u = u_ref[...]
    ext_ref[0:POOL_HALO, :] = jnp.where(i > 0, halo_ref[...], 0.0)
    ext_ref[POOL_HALO:, :] = u
    t_glob = i * tb + lax.broadcasted_iota(jnp.int32, (tb, 1), 0)
    parts = []
    for g, win in enumerate(POOL_WINDOWS):
        cols = slice(g * POOL_GROUP, (g + 1) * POOL_GROUP)
        s = u[:, cols]
        for j in range(1, win):
            s = s + ext_ref[POOL_HALO - j:POOL_HALO - j + tb, cols]
        cnt = jnp.minimum(t_glob + 1, win).astype(F32)
        diff = s / cnt - u[:, cols]
        parts.append(_dot(diff.astype(BF16), pw_ref[g]))
    y_pool = jnp.concatenate(parts, axis=1) * ps_ref[...]
    mix = _dot(y_pool.astype(BF16), wo_ref[:POOL_WIDTH, :]) + _dot(ysb_ref[...], wo_ref[POOL_WIDTH:, :])
    o_ref[...] = _layer_norm(DN_ALPHA * x_ref[...] + mix, g_ref[...], b_ref[...])


def _mix0(u, y_sb, x, pool_w, pool_scale, w_out, ln_g, ln_b, *, tb):
    n = x.shape[0]
    halo_blocks = tb // POOL_HALO
    row = lambda i: (i, 0)
    full = lambda a: pl.BlockSpec(a.shape, lambda i: (0,) * a.ndim)
    return pl.pallas_call(
        functools.partial(_mix0_kernel, tb=tb),
        grid=(n // tb,),
        in_specs=[pl.BlockSpec((tb, POOL_WIDTH), row),
                  pl.BlockSpec((POOL_HALO, POOL_WIDTH), lambda i: (jnp.maximum(i * halo_blocks - 1, 0), 0)),
                  pl.BlockSpec((tb, SB_WIDTH), row),
                  pl.BlockSpec((tb, D_MODEL), row),
                  full(pool_w), full(pool_scale), full(w_out), full(ln_g), full(ln_b)],
        out_specs=pl.BlockSpec((tb, D_MODEL), row),
        out_shape=jax.ShapeDtypeStruct((n, D_MODEL), F32),
        scratch_shapes=[pltpu.VMEM((tb + POOL_HALO, POOL_WIDTH), F32)],
        compiler_params=pltpu.CompilerParams(dimension_semantics=("parallel",)),
        name="pool_outproj_ln",
    )(u, u, y_sb, x, pool_w, pool_scale, w_out, ln_g, ln_b)


def _first_argmax_hit(v, idx, size):
    m = jnp.max(v, axis=0, keepdims=True)
    first = jnp.min(jnp.where(v == m, idx, size), axis=0, keepdims=True)
    return idx == first


def _router_kernel(x_ref, wrt_ref, eb_ref, gates_ref):
    tb = x_ref.shape[0]
    logits = _dot_nt(wrt_ref[...], x_ref[...], precision=_HIGHEST)
    scores = _sigmoid(logits)
    biased = scores + eb_ref[...]
    neg_inf = jnp.float32(-jnp.inf)

    sub = lax.broadcasted_iota(jnp.int32, (GROUP_SIZE, tb), 0)
    group_scores = []
    for g in range(N_GROUPS):
        bg = biased[g * GROUP_SIZE:(g + 1) * GROUP_SIZE, :]
        m1 = jnp.max(bg, axis=0, keepdims=True)
        rest = jnp.where(_first_argmax_hit(bg, sub, GROUP_SIZE), neg_inf, bg)
        group_scores.append(m1 + jnp.max(rest, axis=0, keepdims=True))
    gs = jnp.concatenate(group_scores, axis=0)

    g_idx = lax.broadcasted_iota(jnp.int32, (N_GROUPS, tb), 0)
    group_on = jnp.zeros((N_GROUPS, tb), jnp.bool_)
    for _ in range(TOPK_GROUPS):
        hit = _first_argmax_hit(gs, g_idx, N_GROUPS)
        group_on = jnp.logical_or(group_on, hit)
        gs = jnp.where(hit, neg_inf, gs)

    masked = jnp.concatenate(
        [jnp.where(group_on[g:g + 1, :], biased[g * GROUP_SIZE:(g + 1) * GROUP_SIZE, :], neg_inf)
         for g in range(N_GROUPS)], axis=0)
    e_idx = lax.broadcasted_iota(jnp.int32, (N_EXPERTS, tb), 0)
    chosen = jnp.zeros((N_EXPERTS, tb), jnp.bool_)
    for _ in range(TOP_K):
        hit = _first_argmax_hit(masked, e_idx, N_EXPERTS)
        chosen = jnp.logical_or(chosen, hit)
        masked = jnp.where(hit, neg_inf, masked)

    sel = jnp.where(chosen, scores, 0.0)
    gates = sel / jnp.sum(sel, axis=0, keepdims=True) * ROUTED_SCALE
    gates_ref[...] = jnp.concatenate([gates, jnp.zeros((LANES - N_EXPERTS, tb), F32)], axis=0).T


def _router(x, w_router_t, e_bias_col, *, tb):
    n = x.shape[0]
    return pl.pallas_call(
        _router_kernel,
        grid=(n // tb,),
        in_specs=[pl.BlockSpec((tb, D_MODEL), lambda i: (i, 0)),
                  pl.BlockSpec(w_router_t.shape, lambda i: (0, 0)),
                  pl.BlockSpec(e_bias_col.shape, lambda i: (0, 0))],
        out_specs=pl.BlockSpec((tb, LANES), lambda i: (i, 0)),
        out_shape=jax.ShapeDtypeStruct((n, LANES), F32),
        compiler_params=pltpu.CompilerParams(dimension_semantics=("parallel",)),
        name="router",
    )(x, w_router_t, e_bias_col)


def _swiglu_hidden(xb, wg, wu):
    hg = _dot(xb, wg)
    return hg * _sigmoid(hg) * _dot(xb, wu)


MOE_EXPERTS_PER_STEP = 4


def _moe_kernel(x_ref, gates_ref, wg_ref, wu_ref, wd_ref, sg_ref, su_ref, sd_ref, g_ref, b_ref, o_ref,
                xb_ref, acc_ref):
    step = pl.program_id(1)

    @pl.when(step == 0)
    def _():
        xb = x_ref[...].astype(BF16)
        xb_ref[...] = xb
        acc_ref[...] = _dot(_swiglu_hidden(xb, sg_ref[...], su_ref[...]).astype(BF16), sd_ref[...])

    xb = xb_ref[...]
    gates = gates_ref[...]
    group = range(MOE_EXPERTS_PER_STEP)
    hidden = [_swiglu_hidden(xb, wg_ref[j].astype(BF16), wu_ref[j].astype(BF16)) for j in group]
    total = acc_ref[...]
    for j in group:
        e = step * MOE_EXPERTS_PER_STEP + j
        gate = jnp.take_along_axis(gates, jnp.full(gates.shape, e, jnp.int32), axis=1)
        act = hidden[j] * jnp.concatenate([gate] * (EXPERT_FF // LANES), axis=1)
        total = total + _dot(act.astype(BF16), wd_ref[j].astype(BF16))
    acc_ref[...] = total

    @pl.when(step == pl.num_programs(1) - 1)
    def _():
        o_ref[...] = _layer_norm(DN_ALPHA * x_ref[...] + acc_ref[...], g_ref[...], b_ref[...])


def _moe(x, gates, layer, wg, wu, wd, sg, su, sd, ln_g, ln_b, *, tb):
    n = x.shape[0]
    group = MOE_EXPERTS_PER_STEP
    tok = lambda i, s: (i, 0)
    exp = lambda i, s: (layer, s, 0, 0)
    full = lambda a: pl.BlockSpec(a.shape, lambda i, s: (0,) * a.ndim)
    return pl.pallas_call(
        _moe_kernel,
        grid=(n // tb, N_EXPERTS // group),
        in_specs=[pl.BlockSpec((tb, D_MODEL), tok),
                  pl.BlockSpec((tb, LANES), tok),
                  pl.BlockSpec((None, group, D_MODEL, EXPERT_FF), exp),
                  pl.BlockSpec((None, group, D_MODEL, EXPERT_FF), exp),
                  pl.BlockSpec((None, group, EXPERT_FF, D_MODEL), exp),
                  full(sg), full(su), full(sd), full(ln_g), full(ln_b)],
        out_specs=pl.BlockSpec((tb, D_MODEL), tok),
        out_shape=jax.ShapeDtypeStruct((n, D_MODEL), F32),
        scratch_shapes=[pltpu.VMEM((tb, D_MODEL), BF16),
                        pltpu.VMEM((tb, D_MODEL), F32)],
        compiler_params=pltpu.CompilerParams(dimension_semantics=("parallel", "arbitrary")),
        name="moe_experts_ln",
    )(x, gates, wg, wu, wd, sg, su, sd, ln_g, ln_b)


def _rwkv_pre_kernel(x_ref, prev_ref, mu_ref, wr_ref, wk_ref, wv_ref, w1_ref, w2_ref, w0_ref,
                     a1_ref, a2_ref, a0_ref, g1_ref, g2_ref, kk_scale_ref, ka_ref, rk_ref,
                     ab_ref, av_ref, kr_ref, arb_ref, bp_ref, kpv_ref, dend_ref, g_ref, bonus_ref,
                     ext_ref, r_ref, ld_ref, k_ref, v_ref, kk_ref, b_ref, *, tb):
    i = pl.program_id(0)
    x = x_ref[...]
    ext_ref[0:SUBLANES, :] = jnp.where(i > 0, prev_ref[...], 0.0)
    ext_ref[SUBLANES:, :] = x
    xx = ext_ref[SUBLANES - 1:SUBLANES - 1 + tb, :] - x
    mix = lambda j: (x + xx * mu_ref[j:j + 1, :]).astype(BF16)
    xr, xw, xk, xv, xa, xg = [mix(j) for j in range(6)]

    r = _dot(xr, wr_ref[...])
    w = w0_ref[...] + _dot(jnp.tanh(_dot(xw, w1_ref[...])).astype(BF16), w2_ref[...])
    w_log = -_softplus(-w) - 0.5
    ld_ref[...] = -jnp.exp(w_log)
    k = _dot(xk, wk_ref[...])
    v = _dot(xv, wv_ref[...])
    a = _sigmoid(a0_ref[...] + _dot(_dot(xa, a1_ref[...]).astype(BF16), a2_ref[...]))
    g_ref[...] = _dot(_sigmoid(_dot(xg, g1_ref[...])).astype(BF16), g2_ref[...]).astype(BF16)

    ones = _head_ones()
    kk = k * kk_scale_ref[...]
    kk = kk / jnp.maximum(jnp.sqrt(_head_sum(kk * kk, ones)), 1e-12)
    k = k * (1.0 + (a - 1.0) * ka_ref[...])
    bonus_ref[...] = (_head_sum(r * k * rk_ref[...], ones) * v).astype(BF16)
    r_ref[...] = r
    k_ref[...] = k
    v_ref[...] = v
    kk_ref[...] = kk
    b_ref[...] = kk * a
    _scan_operands(r_ref, ld_ref, k_ref, v_ref, kk_ref, b_ref,
                   ab_ref, av_ref, kr_ref, arb_ref, bp_ref, kpv_ref, dend_ref, tb=tb)


def _rwkv_pre(x, mu, wr, wk, wv, w1, w2, w0, a1, a2, a0, g1, g2, k_k, k_a, r_k, *, tb):
    n = x.shape[0]
    row = lambda i: (i, 0)
    full = lambda a: pl.BlockSpec(a.shape, lambda i: (0,) * a.ndim)
    params = (mu, wr, wk, wv, w1, w2, w0, a1, a2, a0, g1, g2, k_k, k_a, r_k)
    per_chunk = SUBLANES / SCAN_CHUNK
    one, two, eighth = _scan_spec(1, tb), _scan_spec(2, tb), _scan_spec(per_chunk, tb)
    shape = lambda rows, dt: jax.ShapeDtypeStruct((int(n * rows), D_MODEL), dt)
    return pl.pallas_call(
        functools.partial(_rwkv_pre_kernel, tb=tb),
        grid=(n // tb,),
        in_specs=[pl.BlockSpec((tb, D_MODEL), row),
                  pl.BlockSpec((SUBLANES, D_MODEL), lambda i: (jnp.maximum(i * (tb // SUBLANES) - 1, 0), 0))]
                 + [full(p) for p in params],
        out_specs=[one, two, two, one, one, two, eighth, one, one],
        out_shape=[shape(1, F32), shape(2, F32), shape(2, BF16), shape(1, BF16), shape(1, BF16), shape(2, F32),
                   shape(per_chunk, F32), shape(1, BF16), shape(1, BF16)],
        scratch_shapes=[pltpu.VMEM((tb + SUBLANES, D_MODEL), F32)] + [pltpu.VMEM((tb, D_MODEL), F32)] * 6,
        compiler_params=pltpu.CompilerParams(dimension_semantics=("parallel",)),
        name="rwkv_projections",
    )(x, x, *params)


SCAN_CHUNK = 64
SOLVE_BLOCK = 16
N_PAIRS = D_MODEL // LANES


def _dot_tn(a, b):
    return lax.dot_general(a, b, (((0,), (0,)), ((), ())), preferred_element_type=F32)


def _split3_bf16(x):
    hi = x.astype(BF16)
    r1 = x - hi.astype(F32)
    mid = r1.astype(BF16)
    return hi, mid, (r1 - mid.astype(F32)).astype(BF16)


def _block_diag(x):
    lane = lax.broadcasted_iota(jnp.int32, x.shape, 1)
    zero = jnp.zeros_like(x)
    return jnp.concatenate([jnp.where(lane < RW_HEAD, x, zero), jnp.where(lane >= RW_HEAD, x, zero)], axis=0)


def _same_head_mask():
    r = lax.broadcasted_iota(jnp.int32, (LANES, LANES), 0)
    c = lax.broadcasted_iota(jnp.int32, (LANES, LANES), 1)
    return (r // RW_HEAD) == (c // RW_HEAD)


def _scan_operands(r_ref, ld_ref, k_ref, v_ref, kk_ref, b_ref,
                   ab_ref, av_ref, kr_ref, arb_ref, bp_ref, kpv_ref, dend_ref, *, tb):
    c = SCAN_CHUNK
    row = lax.broadcasted_iota(jnp.int32, (c, LANES), 0)
    li = lax.broadcasted_iota(jnp.int32, (c, LANES), 1) % RW_HEAD
    strict = li < row
    incl = li <= row
    r2 = lax.broadcasted_iota(jnp.int32, (c, c), 0)
    c2 = lax.broadcasted_iota(jnp.int32, (c, c), 1)
    cumsum = (c2 <= r2).astype(BF16)
    same_head = _same_head_mask()

    def chunk(ci, carry):
        rows = pl.ds(pl.multiple_of(ci * c, c), c)
        rows2 = pl.ds(pl.multiple_of(ci * 2 * c, 2 * c), 2 * c)
        rows8 = pl.ds(pl.multiple_of(ci * SUBLANES, SUBLANES), SUBLANES)
        pairs = range(N_PAIRS)
        lanes = [slice(p * LANES, (p + 1) * LANES) for p in pairs]
        ld_step = [ld_ref[rows, lanes[p]] for p in pairs]
        parts = [_split3_bf16(x) for x in ld_step]
        ld = [_dot(cumsum, hi) + _dot(cumsum, mid) + _dot(cumsum, lo) for hi, mid, lo in parts]
        ld_end = [x[c - 1:c, :] for x in ld]
        inv_d = [jnp.exp(-x) for x in ld]
        vb = [v_ref[rows, lanes[p]].astype(BF16) for p in pairs]
        kr = [jnp.concatenate([kk_ref[rows, lanes[p]] * jnp.exp(ld[p] - ld_step[p]),
                               r_ref[rows, lanes[p]] * jnp.exp(ld[p])], axis=0).astype(BF16) for p in pairs]
        a_k = [_dot_nt(kr[p], _block_diag((k_ref[rows, lanes[p]] * inv_d[p]).astype(BF16))) for p in pairs]
        a_b = [_dot_nt(kr[p], _block_diag((b_ref[rows, lanes[p]] * inv_d[p]).astype(BF16))) for p in pairs]
        for p in pairs:
            ab_ref[rows, lanes[p]] = jnp.where(strict, a_b[p][:c], 0.0)
            arb_ref[rows, lanes[p]] = jnp.where(incl, a_b[p][c:], 0.0).astype(BF16)
            kr_ref[rows2, lanes[p]] = kr[p]
        a_k = [jnp.concatenate([jnp.where(strict, x[:c], 0.0), jnp.where(incl, x[c:], 0.0)], axis=0).astype(BF16)
               for x in a_k]
        av = [_dot(a_k[p], _block_diag(vb[p])) for p in pairs]
        to_end = [jnp.exp(ld_end[p] - ld[p]) for p in pairs]
        kpv = [_dot_tn((k_ref[rows, lanes[p]] * to_end[p]).astype(BF16), vb[p]) for p in pairs]
        for p in pairs:
            av_ref[rows2, lanes[p]] = av[p]
            bp_ref[rows, lanes[p]] = (b_ref[rows, lanes[p]] * to_end[p]).astype(BF16)
            kpv_ref[rows2, lanes[p]] = jnp.where(same_head, kpv[p], 0.0)
            dend_ref[rows8, lanes[p]] = jnp.broadcast_to(jnp.exp(ld_end[p]), (SUBLANES, LANES))
        return carry

    lax.fori_loop(0, tb // c, chunk, 0)


def _scan_spec(rows_per_step, tb):
    return pl.BlockSpec((int(tb * rows_per_step), D_MODEL), lambda i: (i, 0))


def _scan_kernel(ab_ref, av_ref, kr_ref, arb_ref, bp_ref, kpv_ref, dend_ref, y_ref, st_ref, u_ref, *, tb):
    c = SCAN_CHUNK
    n_blk = c // SOLVE_BLOCK
    tiles = SOLVE_BLOCK // SUBLANES

    @pl.when(pl.program_id(0) ==0)
    def _():
        st_ref[...] = jnp.zeros_like(st_ref)

    lane8 = lax.broadcasted_iota(jnp.int32, (SUBLANES, LANES), 1)
    rr = lax.broadcasted_iota(jnp.int32, (LANES, LANES), 0)
    cc = lax.broadcasted_iota(jnp.int32, (LANES, LANES), 1)
    same_head = _same_head_mask()
    eye = rr == cc

    def chunk(ci, carry):
        row0 = pl.multiple_of(ci * c, c)
        rows = pl.ds(row0, c)
        row2 = pl.multiple_of(ci * 2 * c, 2 * c)
        rows8 = pl.ds(pl.multiple_of(ci * SUBLANES, SUBLANES), SUBLANES)
        y_state = []
        for p in range(N_PAIRS):
            lanes = slice(p * LANES, (p + 1) * LANES)
            sk = _dot(kr_ref[pl.ds(row2, 2 * c), lanes], st_ref[p].astype(BF16))
            u_ref[p] = sk[:c] + av_ref[pl.ds(row2, c), lanes]
            y_state.append(sk[c:])

        for g in range(n_blk):
            base = g * SOLVE_BLOCK
            cur = [[u_ref[p, base + s * SUBLANES:base + (s + 1) * SUBLANES, :] for s in range(tiles)]
                   for p in range(N_PAIRS)]
            for s in range(tiles):
                for j in range(SUBLANES):
                    i = base + s * SUBLANES + j
                    idx = jnp.where(lane8 >= RW_HEAD, RW_HEAD + i, i)
                    for p in range(N_PAIRS):
                        lanes = slice(p * LANES, (p + 1) * LANES)
                        solved = cur[p][s][j:j + 1, :]
                        for s2 in range(s, tiles):
                            below = ab_ref[pl.ds(row0 + base + s2 * SUBLANES, SUBLANES), lanes]
                            cur[p][s2] = cur[p][s2] - jnp.take_along_axis(below, idx, axis=1) * solved
            n_rest = c - base - SOLVE_BLOCK
            for p in range(N_PAIRS):
                lanes = slice(p * LANES, (p + 1) * LANES)
                ub = jnp.concatenate(cur[p], axis=0)
                u_ref[p, base:base + SOLVE_BLOCK, :] = ub
                if n_rest:
                    ub = ub.astype(BF16)
                    lane_b = lax.broadcasted_iota(jnp.int32, ub.shape, 1)
                    pieces = []
                    for h in range(2):
                        keep = (lane_b >= RW_HEAD) if h else (lane_b < RW_HEAD)
                        if base:
                            pieces.append(jnp.zeros((base, LANES), BF16))
                        pieces.append(jnp.where(keep, ub, jnp.zeros_like(ub)))
                        pieces.append(jnp.zeros((n_rest, LANES), BF16))
                    rest = slice(base + SOLVE_BLOCK, c)
                    lower = ab_ref[pl.ds(row0 + base + SOLVE_BLOCK, n_rest), lanes].astype(BF16)
                    u_ref[p, rest, :] = u_ref[p, rest, :] - _dot(lower, jnp.concatenate(pieces, axis=0))

        for p in range(N_PAIRS):
            lanes = slice(p * LANES, (p + 1) * LANES)
            u = u_ref[p].astype(BF16)
            y_ref[rows, lanes] = (y_state[p] + av_ref[pl.ds(row2 + c, c), lanes]
                                  - _dot(arb_ref[rows, lanes], _block_diag(u)))
            upd = kpv_ref[pl.ds(row2, 2 * c), lanes] - jnp.where(same_head, _dot_tn(bp_ref[rows, lanes], u), 0.0)
            d_end = dend_ref[rows8, lanes][0:1, :]
            d_col = jnp.sum(jnp.where(eye, d_end, 0.0), axis=1, keepdims=True)
            st_ref[p] = st_ref[p] * d_col + upd
        return carry

    lax.fori_loop(0, tb // c, chunk, 0)


def _rwkv_scan(ab, av, kr, arb, bp, kpv, dend, *, tb):
    n = ab.shape[0]
    one, two, eighth = _scan_spec(1, tb), _scan_spec(2, tb), _scan_spec(SUBLANES / SCAN_CHUNK, tb)
    return pl.pallas_call(
        functools.partial(_scan_kernel, tb=tb),
        grid=(n // tb,),
        in_specs=[one, two, two, one, one, two, eighth],
        out_specs=one,
        out_shape=jax.ShapeDtypeStruct((n, D_MODEL), F32),
        scratch_shapes=[pltpu.VMEM((N_PAIRS, LANES, LANES), F32),
                        pltpu.VMEM((N_PAIRS, SCAN_CHUNK, LANES), F32)],
        compiler_params=pltpu.CompilerParams(dimension_semantics=("arbitrary",)),
        name="rwkv_scan",
    )(ab, av, kr, arb, bp, kpv, dend)


def _rwkv_post_kernel(y_ref, g_ref, bonus_ref, x_ref, lg_ref, lb_ref, wo_ref, ng_ref, nb_ref, o_ref):
    ones = _head_ones()
    y = y_ref[...]
    inv_n = 1.0 / RW_HEAD
    c = y - _head_sum(y, ones) * inv_n
    var = _head_sum(c * c, ones) * inv_n
    yn = c * lax.rsqrt(var + GN_EPS) * lg_ref[...] + lb_ref[...]
    out = _dot(((yn + bonus_ref[...].astype(F32)) * g_ref[...].astype(F32)).astype(BF16), wo_ref[...])
    o_ref[...] = _layer_norm(DN_ALPHA * x_ref[...] + out, ng_ref[...], nb_ref[...])


def _rwkv_post(y, g, bonus, x, lnx_g, lnx_b, w_o, ln_g, ln_b, *, tb):
    n = x.shape[0]
    row = pl.BlockSpec((tb, D_MODEL), lambda i: (i, 0))
    full = lambda a: pl.BlockSpec(a.shape, lambda i: (0,) * a.ndim)
    params = (lnx_g, lnx_b, w_o, ln_g, ln_b)
    return pl.pallas_call(
        _rwkv_post_kernel,
        grid=(n // tb,),
        in_specs=[row] * 4 + [full(p) for p in params],
        out_specs=row,
        out_shape=jax.ShapeDtypeStruct((n, D_MODEL), F32),
        compiler_params=pltpu.CompilerParams(dimension_semantics=("parallel",)),
        name="rwkv_output_ln",
    )(y, g, bonus, x, *params)


def _row(a):
    return a.reshape(1, -1)


def _moe_layer(x, layer, w_router, e_bias, w_gate, w_up, w_down, ws_gate, ws_up, ws_down, ln_g, ln_b):
    n = x.shape[0]
    gates = _router(x, w_router[layer].T, e_bias[layer].reshape(-1, 1), tb=min(512, n))
    return _moe(x, gates, layer, w_gate, w_up, w_down,
                ws_gate[layer].astype(BF16), ws_up[layer].astype(BF16), ws_down[layer].astype(BF16),
                _row(ln_g[layer]), _row(ln_b[layer]), tb=min(1024, n))


def kernel(x, ln_mix_g, ln_mix_b, ln_ffn_g, ln_ffn_b, ab_w_in, ab_pool_w, ab_pool_scale, ab_w_out, rw_mu, rw_w_r, rw_w_k, rw_w_v, rw_w_w1, rw_w_w2, rw_w0, rw_a1, rw_a2, rw_a0, rw_g1, rw_g2, rw_k_k, rw_k_a, rw_r_k, rw_lnx_g, rw_lnx_b, rw_w_o, moe_w_router, moe_e_bias, moe_w_gate, moe_w_up, moe_w_down, moe_ws_gate, moe_ws_up, moe_ws_down):
    batch, seq, d = x.shape
    assert batch == 1 and d == D_MODEL and seq % SCAN_CHUNK == 0
    n = seq
    h = x.reshape(n, d)
    moe_args = (moe_w_router, moe_e_bias, moe_w_gate, moe_w_up, moe_w_down, moe_ws_gate, moe_ws_up, moe_ws_down)

    u, qkv = _inproj(h, ab_w_in[0].astype(BF16), tm=min(512, n))
    y_sb = _stick_breaking(qkv, tq=min(512, n))
    h = _mix0(u, y_sb, h, ab_pool_w[0].astype(BF16), _row(ab_pool_scale[0]), ab_w_out[0].astype(BF16),
              _row(ln_mix_g[0]), _row(ln_mix_b[0]), tb=min(256, n))
    h = _moe_layer(h, 0, *moe_args, ln_ffn_g, ln_ffn_b)

    bf = lambda a: a[0].astype(BF16)
    *scan_operands, g, bonus = _rwkv_pre(
        h, rw_mu[0], bf(rw_w_r), bf(rw_w_k), bf(rw_w_v), bf(rw_w_w1), bf(rw_w_w2), _row(rw_w0[0]),
        bf(rw_a1), bf(rw_a2), _row(rw_a0[0]), bf(rw_g1), bf(rw_g2), _row(rw_k_k[0]), _row(rw_k_a[0]),
        _row(rw_r_k[0]), tb=min(256, n))
    y = _rwkv_scan(*scan_operands, tb=min(256, n))
    h = _rwkv_post(y, g, bonus, h, _row(rw_lnx_g[0]), _row(rw_lnx_b[0]), bf(rw_w_o),
                   _row(ln_mix_g[1]), _row(ln_mix_b[1]), tb=min(256, n))
    h = _moe_layer(h, 1, *moe_args, ln_ffn_g, ln_ffn_b)
    return h.reshape(batch, seq, d)
```

```python
import functools

import jax
import jax.numpy as jnp
from jax import lax
from jax.experimental import pallas as pl
from jax.experimental.pallas import tpu as pltpu

F32 = jnp.float32
BF16 = jnp.bfloat16

D_MODEL = 1024
DEPTH = 2
POOL_WINDOWS = (2, 4, 8, 16)
POOL_WIDTH = 512
POOL_GROUP = 128
POOL_HALO = 16
SB_HEAD_DIM = 64
SB_WIDTH = 512
RW_HEAD = 64
GN_EPS = 64e-5
N_EXPERTS = 64
TOP_K = 8
N_GROUPS = 8
GROUP_SIZE = N_EXPERTS // N_GROUPS
TOPK_GROUPS = 4
EXPERT_FF = 256
ROUTED_SCALE = 2.5
DN_ALPHA = (2.0 * DEPTH) ** 0.25
LN_EPS = 1e-5

LANES = 128
SUBLANES = 8
EXP_ZERO_BOUND = 110.0


def _dot(a, b, **kw):
    return jnp.dot(a, b, preferred_element_type=F32, **kw)


def _dot_nt(a, b, **kw):
    return lax.dot_general(a, b, (((1,), (1,)), ((), ())), preferred_element_type=F32, **kw)


def _split_bf16(x):
    hi = x.astype(BF16)
    lo = (x - hi.astype(F32)).astype(BF16)
    return hi, lo


def _layer_norm(h, g, b):
    mu = jnp.mean(h, axis=-1, keepdims=True)
    c = h - mu
    var = jnp.mean(c * c, axis=-1, keepdims=True)
    return c * lax.rsqrt(var + LN_EPS) * g + b


def _sigmoid(x):
    return 1.0 / (1.0 + jnp.exp(-x))


def _softplus(x):
    return jnp.maximum(x, 0.0) + jnp.log(1.0 + jnp.exp(-jnp.abs(x)))


def _head_ones():
    r = lax.broadcasted_iota(jnp.int32, (LANES, LANES), 0)
    c = lax.broadcasted_iota(jnp.int32, (LANES, LANES), 1)
    return ((r // RW_HEAD) == (c // RW_HEAD)).astype(BF16)


def _head_sum(x, ones):
    hi, lo = _split_bf16(x)
    outs = []
    for j in range(x.shape[1] // LANES):
        sl = slice(j * LANES, (j + 1) * LANES)
        outs.append(_dot(hi[:, sl], ones) + _dot(lo[:, sl], ones))
    return jnp.concatenate(outs, axis=1)


def _inproj_kernel(x_ref, w_ref, u_ref, qkv_ref):
    xb = x_ref[...].astype(BF16)
    u_ref[...] = _dot(xb, w_ref[:, :POOL_WIDTH])
    qkv_ref[...] = _dot(xb, w_ref[:, POOL_WIDTH:]).astype(BF16)


def _inproj(x, w_in, *, tm):
    n = x.shape[0]
    return pl.pallas_call(
        _inproj_kernel,
        grid=(n // tm,),
        in_specs=[pl.BlockSpec((tm, D_MODEL), lambda i: (i, 0)),
                  pl.BlockSpec(w_in.shape, lambda i: (0, 0))],
        out_specs=[pl.BlockSpec((tm, POOL_WIDTH), lambda i: (i, 0)),
                   pl.BlockSpec((tm, 3 * SB_WIDTH), lambda i: (i, 0))],
        out_shape=[jax.ShapeDtypeStruct((n, POOL_WIDTH), F32),
                   jax.ShapeDtypeStruct((n, 3 * SB_WIDTH), BF16)],
        compiler_params=pltpu.CompilerParams(dimension_semantics=("parallel",)),
        name="inproj",
    )(x, w_in)


SB_TILE = 128


def _sb_kernel(q_ref, k_ref, v_ref, o_ref, carry_ref, acc_ref):
    t = SB_TILE
    n_sub = q_ref.shape[0] // t
    first = pl.program_id(1) * n_sub
    lane = lax.broadcasted_iota(jnp.int32, (t, LANES), 1)
    row = lax.broadcasted_iota(jnp.int32, (t, t), 0)
    col = lax.broadcasted_iota(jnp.int32, (t, t), 1)
    sums = jnp.concatenate([(row > col).astype(BF16), jnp.ones((t, t), BF16)], axis=1)
    scale = SB_HEAD_DIM ** -0.5
    neg_inf = jnp.float32(-jnp.inf)
    heads = (lane < SB_HEAD_DIM, lane >= SB_HEAD_DIM)

    carry_ref[...] = jnp.zeros_like(carry_ref)
    acc_ref[...] = jnp.zeros_like(acc_ref)

    def body(st):
        m, _ = st
        walks = [(s, h) for s in range(n_sub) for h in range(2)]
        j = [first + s - m for s in range(n_sub)]
        jc = [jnp.maximum(x, 0) for x in j]
        k0 = [pl.multiple_of(x * t, t) for x in jc]
        past = [col + (jc[s] - first - s) * t < row for s in range(n_sub)]
        z = [_dot_nt(jnp.where(heads[h], q_ref[s * t:(s + 1) * t, :], 0), k_ref[pl.ds(k0[s], t), :]) * scale
             for s, h in walks]
        sp = [_softplus(-x) for x in z]
        log_fail = [jnp.where(past[s], -sp[c] - z[c], 0.0) for c, (s, h) in enumerate(walks)]
        split = [_split_bf16(x) for x in log_fail]
        part = [_dot(hi, sums) + _dot(lo, sums) for hi, lo in split]
        w = [jnp.where(past[s], jnp.exp(part[c][:, :t] + carry_ref[c] - sp[c]), 0.0)
             for c, (s, h) in enumerate(walks)]
        pv = [_dot(w[c].astype(BF16), v_ref[pl.ds(k0[s], t), :]) for c, (s, h) in enumerate(walks)]
        live = jnp.full((t, t), neg_inf, F32)
        for c, (s, h) in enumerate(walks):
            acc_ref[c] += jnp.where(j[s] >= 0, pv[c], 0.0)
            carry = carry_ref[c] + part[c][:, t:]
            carry_ref[c] = carry
            live = jnp.maximum(live, jnp.where(j[s] >= 1, carry, neg_inf))
        return m + 1, jnp.max(live)

    lax.while_loop(lambda st: st[1] > -EXP_ZERO_BOUND, body, (jnp.int32(0), jnp.float32(0.0)))
    for s in range(n_sub):
        o_ref[s * t:(s + 1) * t, :] = jnp.where(heads[0], acc_ref[2 * s], acc_ref[2 * s + 1]).astype(BF16)


def _stick_breaking(qkv, *, tq):
    n = qkv.shape[0]
    n_pairs = SB_WIDTH // LANES
    n_walks = 2 * tq // SB_TILE
    return pl.pallas_call(
        _sb_kernel,
        grid=(n_pairs, n // tq),
        in_specs=[pl.BlockSpec((tq, LANES), lambda p, i: (i, p)),
                  pl.BlockSpec((n, LANES), lambda p, i: (0, n_pairs + p)),
                  pl.BlockSpec((n, LANES), lambda p, i: (0, 2 * n_pairs + p))],
        out_specs=pl.BlockSpec((tq, LANES), lambda p, i: (i, p)),
        out_shape=jax.ShapeDtypeStruct((n, SB_WIDTH), BF16),
        scratch_shapes=[pltpu.VMEM((n_walks, SB_TILE, SB_TILE), F32),
                        pltpu.VMEM((n_walks, SB_TILE, LANES), F32)],
        compiler_params=pltpu.CompilerParams(dimension_semantics=("parallel", "parallel")),
        name="stick_breaking",
    )(qkv, qkv, qkv)


def _mix0_kernel(u_ref, halo_ref, ysb_ref, x_ref, pw_ref, ps_ref, wo_ref, g_ref, b_ref, o_ref, ext_ref, *, tb):
    i = pl.program_id(0)
    u = u_ref[...]
    ext_ref[0:POOL_HALO, :] = jnp.where(i > 0, halo_ref[...], 0.0)
    ext_ref[POOL_HALO:, :] = u
    t_glob = i * tb + lax.broadcasted_iota(jnp.int32, (tb, 1), 0)
    parts = []
    for g, win in enumerate(POOL_WINDOWS):
        cols = slice(g * POOL_GROUP, (g + 1) * POOL_GROUP)
        s = u[:, cols]
        for j in range(1, win):
            s = s + ext_ref[POOL_HALO - j:POOL_HALO - j + tb, cols]
        cnt = jnp.minimum(t_glob + 1, win).astype(F32)
        diff = s / cnt - u[:, cols]
        parts.append(_dot(diff.astype(BF16), pw_ref[g]))
    y_pool = jnp.concatenate(parts, axis=1) * ps_ref[...]
    mix = _dot(y_pool.astype(BF16), wo_ref[:POOL_WIDTH, :]) + _dot(ysb_ref[...], wo_ref[POOL_WIDTH:, :])
    o_ref[...] = _layer_norm(DN_ALPHA * x_ref[...] + mix, g_ref[...], b_ref[...])


def _mix0(u, y_sb, x, pool_w, pool_scale, w_out, ln_g, ln_b, *, tb):
    n = x.shape[0]
    halo_blocks = tb // POOL_HALO
    row = lambda i: (i, 0)
    full = lambda a: pl.BlockSpec(a.shape, lambda i: (0,) * a.ndim)
    return pl.pallas_call(
        functools.partial(_mix0_kernel, tb=tb),
        grid=(n // tb,),
        in_specs=[pl.BlockSpec((tb, POOL_WIDTH), row),
                  pl.BlockSpec((POOL_HALO, POOL_WIDTH), lambda i: (jnp.maximum(i * halo_blocks - 1, 0), 0)),
                  pl.BlockSpec((tb, SB_WIDTH), row),
                  pl.BlockSpec((tb, D_MODEL), row),
                  full(pool_w), full(pool_scale), full(w_out), full(ln_g), full(ln_b)],
        out_specs=pl.BlockSpec((tb, D_MODEL), row),
        out_shape=jax.ShapeDtypeStruct((n, D_MODEL), F32),
        scratch_shapes=[pltpu.VMEM((tb + POOL_HALO, POOL_WIDTH), F32)],
        compiler_params=pltpu.CompilerParams(dimension_semantics=("parallel",)),
        name="pool_outproj_ln",
    )(u, u, y_sb, x, pool_w, pool_scale, w_out, ln_g, ln_b)


def _first_argmax_hit(v, idx, size):
    m = jnp.max(v, axis=0, keepdims=True)
    first = jnp.min(jnp.where(v == m, idx, size), axis=0, keepdims=True)
    return idx == first


def _router_kernel(x_ref, wrt_ref, eb_ref, gates_ref):
    tb = x_ref.shape[0]
    wh, wl = _split_bf16(wrt_ref[...])
    xh, xl = _split_bf16(x_ref[...])
    logits = _dot_nt(wh, xh) + _dot_nt(wh, xl) + _dot_nt(wl, xh)
    scores = _sigmoid(logits)
    biased = scores + eb_ref[...]
    neg_inf = jnp.float32(-jnp.inf)

    sub = lax.broadcasted_iota(jnp.int32, (GROUP_SIZE, tb), 0)
    group_scores = []
    for g in range(N_GROUPS):
        bg = biased[g * GROUP_SIZE:(g + 1) * GROUP_SIZE, :]
        m1 = jnp.max(bg, axis=0, keepdims=True)
        rest = jnp.where(_first_argmax_hit(bg, sub, GROUP_SIZE), neg_inf, bg)
        group_scores.append(m1 + jnp.max(rest, axis=0, keepdims=True))
    gs = jnp.concatenate(group_scores, axis=0)

    g_idx = lax.broadcasted_iota(jnp.int32, (N_GROUPS, tb), 0)
    group_on = jnp.zeros((N_GROUPS, tb), jnp.bool_)
    for _ in range(TOPK_GROUPS):
        hit = _first_argmax_hit(gs, g_idx, N_GROUPS)
        group_on = jnp.logical_or(group_on, hit)
        gs = jnp.where(hit, neg_inf, gs)

    masked = jnp.concatenate(
        [jnp.where(group_on[g:g + 1, :], biased[g * GROUP_SIZE:(g + 1) * GROUP_SIZE, :], neg_inf)
         for g in range(N_GROUPS)], axis=0)
    e_idx = lax.broadcasted_iota(jnp.int32, (N_EXPERTS, tb), 0)
    chosen = jnp.zeros((N_EXPERTS, tb), jnp.bool_)
    for _ in range(TOP_K):
        hit = _first_argmax_hit(masked, e_idx, N_EXPERTS)
        chosen = jnp.logical_or(chosen, hit)
        masked = jnp.where(hit, neg_inf, masked)

    sel = jnp.where(chosen, scores, 0.0)
    gates = sel / jnp.sum(sel, axis=0, keepdims=True) * ROUTED_SCALE
    gates_ref[...] = jnp.concatenate([gates, jnp.zeros((LANES - N_EXPERTS, tb), F32)], axis=0).T


def _router(x, w_router_t, e_bias_col, *, tb):
    n = x.shape[0]
    return pl.pallas_call(
        _router_kernel,
        grid=(n // tb,),
        in_specs=[pl.BlockSpec((tb, D_MODEL), lambda i: (i, 0)),
                  pl.BlockSpec(w_router_t.shape, lambda i: (0, 0)),
                  pl.BlockSpec(e_bias_col.shape, lambda i: (0, 0))],
        out_specs=pl.BlockSpec((tb, LANES), lambda i: (i, 0)),
        out_shape=jax.ShapeDtypeStruct((n, LANES), F32),
        compiler_params=pltpu.CompilerParams(dimension_semantics=("parallel",)),
        name="router",
    )(x, w_router_t, e_bias_col)


def _swiglu_hidden(xb, wg, wu):
    hg = _dot(xb, wg)
    return hg * _sigmoid(hg) * _dot(xb, wu)


MOE_EXPERTS_PER_STEP = 4


def _moe_kernel(x_ref, gates_ref, wg_ref, wu_ref, wd_ref, sg_ref, su_ref, sd_ref, g_ref, b_ref, o_ref,
                xb_ref, acc_ref):
    step = pl.program_id(1)

    @pl.when(step == 0)
    def _():
        xb = x_ref[...].astype(BF16)
        xb_ref[...] = xb
        acc_ref[...] = _dot(_swiglu_hidden(xb, sg_ref[...], su_ref[...]).astype(BF16), sd_ref[...])

    xb = xb_ref[...]
    gates = gates_ref[...]
    group = range(MOE_EXPERTS_PER_STEP)
    hidden = [_swiglu_hidden(xb, wg_ref[j].astype(BF16), wu_ref[j].astype(BF16)) for j in group]
    total = acc_ref[...]
    for j in group:
        e = step * MOE_EXPERTS_PER_STEP + j
        gate = jnp.take_along_axis(gates, jnp.full(gates.shape, e, jnp.int32), axis=1)
        act = hidden[j] * jnp.concatenate([gate] * (EXPERT_FF // LANES), axis=1)
        total = total + _dot(act.astype(BF16), wd_ref[j].astype(BF16))
    acc_ref[...] = total

    @pl.when(step == pl.num_programs(1) - 1)
    def _():
        o_ref[...] = _layer_norm(DN_ALPHA * x_ref[...] + acc_ref[...], g_ref[...], b_ref[...])


def _moe(x, gates, layer, wg, wu, wd, sg, su, sd, ln_g, ln_b, *, tb):
    n = x.shape[0]
    group = MOE_EXPERTS_PER_STEP
    tok = lambda i, s: (i, 0)
    exp = lambda i, s: (layer, s, 0, 0)
    full = lambda a: pl.BlockSpec(a.shape, lambda i, s: (0,) * a.ndim)
    return pl.pallas_call(
        _moe_kernel,
        grid=(n // tb, N_EXPERTS // group),
        in_specs=[pl.BlockSpec((tb, D_MODEL), tok),
                  pl.BlockSpec((tb, LANES), tok),
                  pl.BlockSpec((None, group, D_MODEL, EXPERT_FF), exp),
                  pl.BlockSpec((None, group, D_MODEL, EXPERT_FF), exp),
                  pl.BlockSpec((None, group, EXPERT_FF, D_MODEL), exp),
                  full(sg), full(su), full(sd), full(ln_g), full(ln_b)],
        out_specs=pl.BlockSpec((tb, D_MODEL), tok),
        out_shape=jax.ShapeDtypeStruct((n, D_MODEL), F32),
        scratch_shapes=[pltpu.VMEM((tb, D_MODEL), BF16),
                        pltpu.VMEM((tb, D_MODEL), F32)],
        compiler_params=pltpu.CompilerParams(dimension_semantics=("parallel", "arbitrary")),
        name="moe_experts_ln",
    )(x, gates, wg, wu, wd, sg, su, sd, ln_g, ln_b)


def _rwkv_pre_kernel(x_ref, prev_ref, mu_ref, wr_ref, wk_ref, wv_ref, w1_ref, w2_ref, w0_ref,
                     a1_ref, a2_ref, a0_ref, g1_ref, g2_ref, kk_scale_ref, ka_ref, rk_ref,
                     ab_ref, av_ref, kr_ref, arb_ref, bp_ref, kpv_ref, dend_ref, g_ref, bonus_ref,
                     ext_ref, r_ref, ld_ref, k_ref, v_ref, kk_ref, b_ref, *, tb):
    i = pl.program_id(0)
    x = x_ref[...]
    ext_ref[0:SUBLANES, :] = jnp.where(i > 0, prev_ref[...], 0.0)
    ext_ref[SUBLANES:, :] = x
    xx = ext_ref[SUBLANES - 1:SUBLANES - 1 + tb, :] - x
    mix = lambda j: (x + xx * mu_ref[j:j + 1, :]).astype(BF16)
    xr, xw, xk, xv, xa, xg = [mix(j) for j in range(6)]

    r = _dot(xr, wr_ref[...])
    w = w0_ref[...] + _dot(jnp.tanh(_dot(xw, w1_ref[...])).astype(BF16), w2_ref[...])
    w_log = -_softplus(-w) - 0.5
    ld_ref[...] = -jnp.exp(w_log)
    k = _dot(xk, wk_ref[...])
    v = _dot(xv, wv_ref[...])
    a = _sigmoid(a0_ref[...] + _dot(_dot(xa, a1_ref[...]).astype(BF16), a2_ref[...]))
    g_ref[...] = _dot(_sigmoid(_dot(xg, g1_ref[...])).astype(BF16), g2_ref[...]).astype(BF16)

    ones = _head_ones()
    kk = k * kk_scale_ref[...]
    kk = kk / jnp.maximum(jnp.sqrt(_head_sum(kk * kk, ones)), 1e-12)
    k = k * (1.0 + (a - 1.0) * ka_ref[...])
    bonus_ref[...] = (_head_sum(r * k * rk_ref[...], ones) * v).astype(BF16)
    r_ref[...] = r
    k_ref[...] = k
    v_ref[...] = v
    kk_ref[...] = kk
    b_ref[...] = kk * a
    _scan_operands(r_ref, ld_ref, k_ref, v_ref, kk_ref, b_ref,
                   ab_ref, av_ref, kr_ref, arb_ref, bp_ref, kpv_ref, dend_ref, tb=tb)


def _rwkv_pre(x, mu, wr, wk, wv, w1, w2, w0, a1, a2, a0, g1, g2, k_k, k_a, r_k, *, tb):
    n = x.shape[0]
    row = lambda i: (i, 0)
    full = lambda a: pl.BlockSpec(a.shape, lambda i: (0,) * a.ndim)
    params = (mu, wr, wk, wv, w1, w2, w0, a1, a2, a0, g1, g2, k_k, k_a, r_k)
    per_chunk = SUBLANES / SCAN_CHUNK
    one, two, eighth = _scan_spec(1, tb), _scan_spec(2, tb), _scan_spec(per_chunk, tb)
    shape = lambda rows, dt: jax.ShapeDtypeStruct((int(n * rows), D_MODEL), dt)
    return pl.pallas_call(
        functools.partial(_rwkv_pre_kernel, tb=tb),
        grid=(n // tb,),
        in_specs=[pl.BlockSpec((tb, D_MODEL), row),
                  pl.BlockSpec((SUBLANES, D_MODEL), lambda i: (jnp.maximum(i * (tb // SUBLANES) - 1, 0), 0))]
                 + [full(p) for p in params],
        out_specs=[one, two, two, one, one, two, eighth, one, one],
        out_shape=[shape(1, F32), shape(2, F32), shape(2, BF16), shape(1, BF16), shape(1, BF16), shape(2, F32),
                   shape(per_chunk, F32), shape(1, BF16), shape(1, BF16)],
        scratch_shapes=[pltpu.VMEM((tb + SUBLANES, D_MODEL), F32)] + [pltpu.VMEM((tb, D_MODEL), F32)] * 6,
        compiler_params=pltpu.CompilerParams(dimension_semantics=("parallel",)),
        name="rwkv_projections",
    )(x, x, *params)


SCAN_CHUNK = 64
SOLVE_BLOCK = 16
N_PAIRS = D_MODEL // LANES


def _dot_tn(a, b):
    return lax.dot_general(a, b, (((0,), (0,)), ((), ())), preferred_element_type=F32)


def _split3_bf16(x):
    hi = x.astype(BF16)
    r1 = x - hi.astype(F32)
    mid = r1.astype(BF16)
    return hi, mid, (r1 - mid.astype(F32)).astype(BF16)


def _block_diag(x):
    lane = lax.broadcasted_iota(jnp.int32, x.shape, 1)
    zero = jnp.zeros_like(x)
    return jnp.concatenate([jnp.where(lane < RW_HEAD, x, zero), jnp.where(lane >= RW_HEAD, x, zero)], axis=0)


def _same_head_mask():
    r = lax.broadcasted_iota(jnp.int32, (LANES, LANES), 0)
    c = lax.broadcasted_iota(jnp.int32, (LANES, LANES), 1)
    return (r // RW_HEAD) == (c // RW_HEAD)


def _scan_operands(r_ref, ld_ref, k_ref, v_ref, kk_ref, b_ref,
                   ab_ref, av_ref, kr_ref, arb_ref, bp_ref, kpv_ref, dend_ref, *, tb):
    c = SCAN_CHUNK
    row = lax.broadcasted_iota(jnp.int32, (c, LANES), 0)
    li = lax.broadcasted_iota(jnp.int32, (c, LANES), 1) % RW_HEAD
    strict = li < row
    incl = li <= row
    r2 = lax.broadcasted_iota(jnp.int32, (c, c), 0)
    c2 = lax.broadcasted_iota(jnp.int32, (c, c), 1)
    cumsum = (c2 <= r2).astype(BF16)
    same_head = _same_head_mask()

    def chunk(ci, carry):
        rows = pl.ds(pl.multiple_of(ci * c, c), c)
        rows2 = pl.ds(pl.multiple_of(ci * 2 * c, 2 * c), 2 * c)
        rows8 = pl.ds(pl.multiple_of(ci * SUBLANES, SUBLANES), SUBLANES)
        pairs = range(N_PAIRS)
        lanes = [slice(p * LANES, (p + 1) * LANES) for p in pairs]
        ld_step = [ld_ref[rows, lanes[p]] for p in pairs]
        parts = [_split3_bf16(x) for x in ld_step]
        ld = [_dot(cumsum, hi) + _dot(cumsum, mid) + _dot(cumsum, lo) for hi, mid, lo in parts]
        ld_end = [x[c - 1:c, :] for x in ld]
        inv_d = [jnp.exp(-x) for x in ld]
        vb = [v_ref[rows, lanes[p]].astype(BF16) for p in pairs]
        kr = [jnp.concatenate([kk_ref[rows, lanes[p]] * jnp.exp(ld[p] - ld_step[p]),
                               r_ref[rows, lanes[p]] * jnp.exp(ld[p])], axis=0).astype(BF16) for p in pairs]
        a_k = [_dot_nt(kr[p], _block_diag((k_ref[rows, lanes[p]] * inv_d[p]).astype(BF16))) for p in pairs]
        a_b = [_dot_nt(kr[p], _block_diag((b_ref[rows, lanes[p]] * inv_d[p]).astype(BF16))) for p in pairs]
        for p in pairs:
            ab_ref[rows, lanes[p]] = jnp.where(strict, a_b[p][:c], 0.0)
            arb_ref[rows, lanes[p]] = jnp.where(incl, a_b[p][c:], 0.0).astype(BF16)
            kr_ref[rows2, lanes[p]] = kr[p]
        a_k = [jnp.concatenate([jnp.where(strict, x[:c], 0.0), jnp.where(incl, x[c:], 0.0)], axis=0).astype(BF16)
               for x in a_k]
        av = [_dot(a_k[p], _block_diag(vb[p])) for p in pairs]
        to_end = [jnp.exp(ld_end[p] - ld[p]) for p in pairs]
        kpv = [_dot_tn((k_ref[rows, lanes[p]] * to_end[p]).astype(BF16), vb[p]) for p in pairs]
        for p in pairs:
            av_ref[rows2, lanes[p]] = av[p]
            bp_ref[rows, lanes[p]] = (b_ref[rows, lanes[p]] * to_end[p]).astype(BF16)
            kpv_ref[rows2, lanes[p]] = jnp.where(same_head, kpv[p], 0.0)
            dend_ref[rows8, lanes[p]] = jnp.broadcast_to(jnp.exp(ld_end[p]), (SUBLANES, LANES))
        return carry

    lax.fori_loop(0, tb // c, chunk, 0)


def _scan_spec(rows_per_step, tb):
    return pl.BlockSpec((int(tb * rows_per_step), D_MODEL), lambda i: (i, 0))


def _scan_kernel(ab_ref, av_ref, kr_ref, arb_ref, bp_ref, kpv_ref, dend_ref, y_ref, st_ref, u_ref, *, tb):
    c = SCAN_CHUNK
    n_blk = c // SOLVE_BLOCK
    tiles = SOLVE_BLOCK // SUBLANES

    @pl.when(pl.program_id(0) == 0)
    def _():
        st_ref[...] = jnp.zeros_like(st_ref)

    lane8 = lax.broadcasted_iota(jnp.int32, (SUBLANES, LANES), 1)
    rr = lax.broadcasted_iota(jnp.int32, (LANES, LANES), 0)
    cc = lax.broadcasted_iota(jnp.int32, (LANES, LANES), 1)
    same_head = _same_head_mask()
    eye = rr == cc

    def chunk(ci, carry):
        row0 = pl.multiple_of(ci * c, c)
        rows = pl.ds(row0, c)
        row2 = pl.multiple_of(ci * 2 * c, 2 * c)
        rows8 = pl.ds(pl.multiple_of(ci * SUBLANES, SUBLANES), SUBLANES)
        y_state = []
        for p in range(N_PAIRS):
            lanes = slice(p * LANES, (p + 1) * LANES)
            sk = _dot(kr_ref[pl.ds(row2, 2 * c), lanes], st_ref[p].astype(BF16))
            u_ref[p] = sk[:c] + av_ref[pl.ds(row2, c), lanes]
            y_state.append(sk[c:])

        for g in range(n_blk):
            base = g * SOLVE_BLOCK
            cur = [[u_ref[p, base + s * SUBLANES:base + (s + 1) * SUBLANES, :] for s in range(tiles)]
                   for p in range(N_PAIRS)]
            for s in range(tiles):
                for j in range(SUBLANES):
                    i = base + s * SUBLANES + j
                    idx = jnp.where(lane8 >= RW_HEAD, RW_HEAD + i, i)
                    for p in range(N_PAIRS):
                        lanes = slice(p * LANES, (p + 1) * LANES)
                        solved = cur[p][s][j:j + 1, :]
                        for s2 in range(s, tiles):
                            below = ab_ref[pl.ds(row0 + base + s2 * SUBLANES, SUBLANES), lanes]
                            cur[p][s2] = cur[p][s2] - jnp.take_along_axis(below, idx, axis=1) * solved
            n_rest = c - base - SOLVE_BLOCK
            for p in range(N_PAIRS):
                lanes = slice(p * LANES, (p + 1) * LANES)
                ub = jnp.concatenate(cur[p], axis=0)
                u_ref[p, base:base + SOLVE_BLOCK, :] = ub
                if n_rest:
                    ub = ub.astype(BF16)
                    lane_b = lax.broadcasted_iota(jnp.int32, ub.shape, 1)
                    pieces = []
                    for h in range(2):
                        keep = (lane_b >= RW_HEAD) if h else (lane_b < RW_HEAD)
                        if base:
                            pieces.append(jnp.zeros((base, LANES), BF16))
                        pieces.append(jnp.where(keep, ub, jnp.zeros_like(ub)))
                        pieces.append(jnp.zeros((n_rest, LANES), BF16))
                    rest = slice(base + SOLVE_BLOCK, c)
                    lower = ab_ref[pl.ds(row0 + base + SOLVE_BLOCK, n_rest), lanes].astype(BF16)
                    u_ref[p, rest, :] = u_ref[p, rest, :] - _dot(lower, jnp.concatenate(pieces, axis=0))

        for p in range(N_PAIRS):
            lanes = slice(p * LANES, (p + 1) * LANES)
            u = u_ref[p].astype(BF16)
            y_ref[rows, lanes] = (y_state[p] + av_ref[pl.ds(row2 + c, c), lanes]
                                  - _dot(arb_ref[rows, lanes], _block_diag(u)))
            upd = kpv_ref[pl.ds(row2, 2 * c), lanes] - jnp.where(same_head, _dot_tn(bp_ref[rows, lanes], u), 0.0)
            d_end = dend_ref[rows8, lanes][0:1, :]
            d_col = jnp.sum(jnp.where(eye, d_end, 0.0), axis=1, keepdims=True)
            st_ref[p] = st_ref[p] * d_col + upd
        return carry

    lax.fori_loop(0, tb // c, chunk, 0)


def _rwkv_scan(ab, av, kr, arb, bp, kpv, dend, *, tb):
    n = ab.shape[0]
    one, two, eighth = _scan_spec(1, tb), _scan_spec(2, tb), _scan_spec(SUBLANES / SCAN_CHUNK, tb)
    return pl.pallas_call(
        functools.partial(_scan_kernel, tb=tb),
        grid=(n // tb,),
        in_specs=[one, two, two, one, one, two, eighth],
        out_specs=one,
        out_shape=jax.ShapeDtypeStruct((n, D_MODEL), F32),
        scratch_shapes=[pltpu.VMEM((N_PAIRS, LANES, LANES), F32),
                        pltpu.VMEM((N_PAIRS, SCAN_CHUNK, LANES), F32)],
        compiler_params=pltpu.CompilerParams(dimension_semantics=("arbitrary",)),
        name="rwkv_scan",
    )(ab, av, kr, arb, bp, kpv, dend)


def _rwkv_post_kernel(y_ref, g_ref, bonus_ref, x_ref, lg_ref, lb_ref, wo_ref, ng_ref, nb_ref, o_ref):
    ones = _head_ones()
    y = y_ref[...]
    inv_n = 1.0 / RW_HEAD
    c = y - _head_sum(y, ones) * inv_n
    var = _head_sum(c * c, ones) * inv_n
    yn = c * lax.rsqrt(var + GN_EPS) * lg_ref[...] + lb_ref[...]
    out = _dot(((yn + bonus_ref[...].astype(F32)) * g_ref[...].astype(F32)).astype(BF16), wo_ref[...])
    o_ref[...] = _layer_norm(DN_ALPHA * x_ref[...] + out, ng_ref[...], nb_ref[...])


def _rwkv_post(y, g, bonus, x, lnx_g, lnx_b, w_o, ln_g, ln_b, *, tb):
    n = x.shape[0]
    row = pl.BlockSpec((tb, D_MODEL), lambda i: (i, 0))
    full = lambda a: pl.BlockSpec(a.shape, lambda i: (0,) * a.ndim)
    params = (lnx_g, lnx_b, w_o, ln_g, ln_b)
    return pl.pallas_call(
        _rwkv_post_kernel,
        grid=(n // tb,),
        in_specs=[row] * 4 + [full(p) for p in params],
        out_specs=row,
        out_shape=jax.ShapeDtypeStruct((n, D_MODEL), F32),
        compiler_params=pltpu.CompilerParams(dimension_semantics=("parallel",)),
        name="rwkv_output_ln",
    )(y, g, bonus, x, *params)


def _row(a):
    return a.reshape(1, -1)


def _block_rows(n):
    rows = dict(inproj=1024, attention=1024, mix0=512, router=1024, moe=1024, rwkv_pre=256, scan=256, rwkv_post=512)
    rows = {name: min(r, n) for name, r in rows.items()}
    assert all(n % r == 0 for r in rows.values()), (n, rows)
    return rows


def _moe_layer(x, layer, rows, w_router, e_bias, w_gate, w_up, w_down, ws_gate, ws_up, ws_down, ln_g, ln_b):
    gates = _router(x, w_router[layer].T, e_bias[layer].reshape(-1, 1), tb=rows["router"])
    return _moe(x, gates, layer, w_gate, w_up, w_down,
                ws_gate[layer].astype(BF16), ws_up[layer].astype(BF16), ws_down[layer].astype(BF16),
                _row(ln_g[layer]), _row(ln_b[layer]), tb=rows["moe"])


def kernel(x, ln_mix_g, ln_mix_b, ln_ffn_g, ln_ffn_b, ab_w_in, ab_pool_w, ab_pool_scale, ab_w_out, rw_mu, rw_w_r, rw_w_k, rw_w_v, rw_w_w1, rw_w_w2, rw_w0, rw_a1, rw_a2, rw_a0, rw_g1, rw_g2, rw_k_k, rw_k_a, rw_r_k, rw_lnx_g, rw_lnx_b, rw_w_o, moe_w_router, moe_e_bias, moe_w_gate, moe_w_up, moe_w_down, moe_ws_gate, moe_ws_up, moe_ws_down):
    batch, seq, d = x.shape
    assert batch == 1 and d == D_MODEL and seq % SCAN_CHUNK == 0
    n = seq
    rows = _block_rows(n)
    h = x.reshape(n, d)
    moe_args = (moe_w_router, moe_e_bias, moe_w_gate, moe_w_up, moe_w_down, moe_ws_gate, moe_ws_up, moe_ws_down)

    u, qkv = _inproj(h, ab_w_in[0].astype(BF16), tm=rows["inproj"])
    y_sb = _stick_breaking(qkv, tq=rows["attention"])
    h = _mix0(u, y_sb, h, ab_pool_w[0].astype(BF16), _row(ab_pool_scale[0]), ab_w_out[0].astype(BF16),
              _row(ln_mix_g[0]), _row(ln_mix_b[0]), tb=rows["mix0"])
    h = _moe_layer(h, 0, rows, *moe_args, ln_ffn_g, ln_ffn_b)

    bf = lambda a: a[0].astype(BF16)
    *scan_operands, g, bonus = _rwkv_pre(
        h, rw_mu[0], bf(rw_w_r), bf(rw_w_k), bf(rw_w_v), bf(rw_w_w1), bf(rw_w_w2), _row(rw_w0[0]),
        bf(rw_a1), bf(rw_a2), _row(rw_a0[0]), bf(rw_g1), bf(rw_g2), _row(rw_k_k[0]), _row(rw_k_a[0]),
        _row(rw_r_k[0]), tb=rows["rwkv_pre"])
    y = _rwkv_scan(*scan_operands, tb=rows["scan"])
    h = _rwkv_post(y, g, bonus, h, _row(rw_lnx_g[0]), _row(rw_lnx_b[0]), bf(rw_w_o),
                   _row(ln_mix_g[1]), _row(ln_mix_b[1]), tb=rows["rwkv_post"])
    h = _moe_layer(h, 1, rows, *moe_args, ln_ffn_g, ln_ffn_b)
    return h.reshape(batch, seq, d)
```

```python
import functools

import jax
import jax.numpy as jnp
from jax import lax
from jax.experimental import pallas as pl
from jax.experimental.pallas import tpu as pltpu

F32 = jnp.float32
BF16 = jnp.bfloat16

D_MODEL = 1024
DEPTH = 2
POOL_WINDOWS = (2, 4, 8, 16)
POOL_WIDTH = 512
POOL_GROUP = 128
POOL_HALO = 16
SB_HEAD_DIM = 64
SB_WIDTH = 512
RW_HEAD = 64
GN_EPS = 64e-5
N_EXPERTS = 64
TOP_K = 8
N_GROUPS = 8
GROUP_SIZE = N_EXPERTS // N_GROUPS
TOPK_GROUPS = 4
EXPERT_FF = 256
ROUTED_SCALE = 2.5
DN_ALPHA = (2.0 * DEPTH) ** 0.25
LN_EPS = 1e-5

LANES = 128
SUBLANES = 8
EXP_ZERO_BOUND = 110.0


def _dot(a, b, **kw):
    return jnp.dot(a, b, preferred_element_type=F32, **kw)


def _dot_nt(a, b, **kw):
    return lax.dot_general(a, b, (((1,), (1,)), ((), ())), preferred_element_type=F32, **kw)


def _split_bf16(x):
    hi = x.astype(BF16)
    lo = (x - hi.astype(F32)).astype(BF16)
    return hi, lo


def _layer_norm(h, g, b):
    mu = jnp.mean(h, axis=-1, keepdims=True)
    c = h - mu
    var = jnp.mean(c * c, axis=-1, keepdims=True)
    return c * lax.rsqrt(var + LN_EPS) * g + b


def _sigmoid(x):
    return 1.0 / (1.0 + jnp.exp(-x))


def _softplus(x):
    return jnp.maximum(x, 0.0) + jnp.log(1.0 + jnp.exp(-jnp.abs(x)))


def _head_ones():
    r = lax.broadcasted_iota(jnp.int32, (LANES, LANES), 0)
    c = lax.broadcasted_iota(jnp.int32, (LANES, LANES), 1)
    return ((r // RW_HEAD) == (c // RW_HEAD)).astype(BF16)


def _head_sum(x, ones):
    hi, lo = _split_bf16(x)
    outs = []
    for j in range(x.shape[1] // LANES):
        sl = slice(j * LANES, (j + 1) * LANES)
        outs.append(_dot(hi[:, sl], ones) + _dot(lo[:, sl], ones))
    return jnp.concatenate(outs, axis=1)


def _inproj_kernel(x_ref, w_ref, u_ref, qkv_ref):
    xb = x_ref[...].astype(BF16)
    u_ref[...] = _dot(xb, w_ref[:, :POOL_WIDTH])
    qkv_ref[...] = _dot(xb, w_ref[:, POOL_WIDTH:]).astype(BF16)


def _inproj(x, w_in, *, tm):
    n = x.shape[0]
    return pl.pallas_call(
        _inproj_kernel,
        grid=(n // tm,),
        in_specs=[pl.BlockSpec((tm, D_MODEL), lambda i: (i, 0)),
                  pl.BlockSpec(w_in.shape, lambda i: (0, 0))],
        out_specs=[pl.BlockSpec((tm, POOL_WIDTH), lambda i: (i, 0)),
                   pl.BlockSpec((tm, 3 * SB_WIDTH), lambda i: (i, 0))],
        out_shape=[jax.ShapeDtypeStruct((n, POOL_WIDTH), F32),
                   jax.ShapeDtypeStruct((n, 3 * SB_WIDTH), BF16)],
        compiler_params=pltpu.CompilerParams(dimension_semantics=("parallel",)),
        name="inproj",
    )(x, w_in)


SB_TILE = 128


def _sb_kernel(q_ref, k_ref, v_ref, o_ref, carry_ref, acc_ref):
    t = SB_TILE
    n_sub = q_ref.shape[0] // t
    first = pl.program_id(1) * n_sub
    lane = lax.broadcasted_iota(jnp.int32, (t, LANES), 1)
    row = lax.broadcasted_iota(jnp.int32, (t, t), 0)
    col = lax.broadcasted_iota(jnp.int32, (t, t), 1)
    sums = jnp.concatenate([(row > col).astype(BF16), jnp.ones((t, t), BF16)], axis=1)
    scale = SB_HEAD_DIM ** -0.5
    neg_inf = jnp.float32(-jnp.inf)
    heads = (lane < SB_HEAD_DIM, lane >= SB_HEAD_DIM)

    carry_ref[...] = jnp.zeros_like(carry_ref)
    acc_ref[...] = jnp.zeros_like(acc_ref)

    def body(st):
        m, _ = st
        walks = [(s, h) for s in range(n_sub) for h in range(2)]
        j = [first + s - m for s in range(n_sub)]
        jc = [jnp.maximum(x, 0) for x in j]
        k0 = [pl.multiple_of(x * t, t) for x in jc]
        past = [col + (jc[s] - first - s) * t < row for s in range(n_sub)]
        z = [_dot_nt(jnp.where(heads[h], q_ref[s * t:(s + 1) * t, :], 0), k_ref[pl.ds(k0[s], t), :]) * scale
             for s, h in walks]
        sp = [_softplus(-x) for x in z]
        log_fail = [jnp.where(past[s], -sp[c] - z[c], 0.0) for c, (s, h) in enumerate(walks)]
        split = [_split_bf16(x) for x in log_fail]
        part = [_dot(hi, sums) + _dot(lo, sums) for hi, lo in split]
        w = [jnp.where(past[s], jnp.exp(part[c][:, :t] + carry_ref[c] - sp[c]), 0.0)
             for c, (s, h) in enumerate(walks)]
        pv = [_dot(w[c].astype(BF16), v_ref[pl.ds(k0[s], t), :]) for c, (s, h) in enumerate(walks)]
        live = jnp.full((t, t), neg_inf, F32)
        for c, (s, h) in enumerate(walks):
            acc_ref[c] += jnp.where(j[s] >= 0, pv[c], 0.0)
            carry = carry_ref[c] + part[c][:, t:]
            carry_ref[c] = carry
            live = jnp.maximum(live, jnp.where(j[s] >= 1, carry, neg_inf))
        return m + 1, jnp.max(live)

    lax.while_loop(lambda st: st[1] > -EXP_ZERO_BOUND, body, (jnp.int32(0), jnp.float32(0.0)))
    for s in range(n_sub):
        o_ref[s * t:(s + 1) * t, :] = jnp.where(heads[0], acc_ref[2 * s], acc_ref[2 * s + 1]).astype(BF16)


def _stick_breaking(qkv, *, tq):
    n = qkv.shape[0]
    n_pairs = SB_WIDTH // LANES
    n_walks = 2 * tq // SB_TILE
    return pl.pallas_call(
        _sb_kernel,
        grid=(n_pairs, n // tq),
        in_specs=[pl.BlockSpec((tq, LANES), lambda p, i: (i, p)),
                  pl.BlockSpec((n, LANES), lambda p, i: (0, n_pairs + p)),
                  pl.BlockSpec((n, LANES), lambda p, i: (0, 2 * n_pairs + p))],
        out_specs=pl.BlockSpec((tq, LANES), lambda p, i: (i, p)),
        out_shape=jax.ShapeDtypeStruct((n, SB_WIDTH), BF16),
        scratch_shapes=[pltpu.VMEM((n_walks, SB_TILE, SB_TILE), F32),
                        pltpu.VMEM((n_walks, SB_TILE, LANES), F32)],
        compiler_params=pltpu.CompilerParams(dimension_semantics=("parallel", "parallel")),
        name="stick_breaking",
    )(qkv, qkv, qkv)


def _mix0_kernel(u_ref, halo_ref, ysb_ref, x_ref, pw_ref, ps_ref, wo_ref, g_ref, b_ref, o_ref, ext_ref, *, tb):
    i = pl.program_id(0)
    u = u_ref[...]
    ext_ref[0:POOL_HALO, :] = jnp.where(i > 0, halo_ref[...], 0.0)
    ext_ref[POOL_HALO:, :] = u
    t_glob = i * tb + lax.broadcasted_iota(jnp.int32, (tb, 1), 0)
    parts = []
    for g, win in enumerate(POOL_WINDOWS):
        cols = slice(g * POOL_GROUP, (g + 1) * POOL_GROUP)
        s = u[:, cols]
        for j in range(1, win):
            s = s + ext_ref[POOL_HALO - j:POOL_HALO - j + tb, cols]
        cnt = jnp.minimum(t_glob + 1, win).astype(F32)
        diff = s / cnt - u[:, cols]
        parts.append(_dot(diff.astype(BF16), pw_ref[g]))
    y_pool = jnp.concatenate(parts, axis=1) * ps_ref[...]
    mix = _dot(y_pool.astype(BF16), wo_ref[:POOL_WIDTH, :]) + _dot(ysb_ref[...], wo_ref[POOL_WIDTH:, :])
    o_ref[...] = _layer_norm(DN_ALPHA * x_ref[...] + mix, g_ref[...], b_ref[...])


def _mix0(u, y_sb, x, pool_w, pool_scale, w_out, ln_g, ln_b, *, tb):
    n = x.shape[0]
    halo_blocks = tb // POOL_HALO
    row = lambda i: (i, 0)
    full = lambda a: pl.BlockSpec(a.shape, lambda i: (0,) * a.ndim)
    return pl.pallas_call(
        functools.partial(_mix0_kernel, tb=tb),
        grid=(n // tb,),
        in_specs=[pl.BlockSpec((tb, POOL_WIDTH), row),
                  pl.BlockSpec((POOL_HALO, POOL_WIDTH), lambda i: (jnp.maximum(i * halo_blocks - 1, 0), 0)),
                  pl.BlockSpec((tb, SB_WIDTH), row),
                  pl.BlockSpec((tb, D_MODEL), row),
                  full(pool_w), full(pool_scale), full(w_out), full(ln_g), full(ln_b)],
        out_specs=pl.BlockSpec((tb, D_MODEL), row),
        out_shape=jax.ShapeDtypeStruct((n, D_MODEL), F32),
        scratch_shapes=[pltpu.VMEM((tb + POOL_HALO, POOL_WIDTH), F32)],
        compiler_params=pltpu.CompilerParams(dimension_semantics=("parallel",)),
        name="pool_outproj_ln",
    )(u, u, y_sb, x, pool_w, pool_scale, w_out, ln_g, ln_b)


def _first_argmax_hit(v, idx, size):
    m = jnp.max(v, axis=0, keepdims=True)
    first = jnp.min(jnp.where(v == m, idx, size), axis=0, keepdims=True)
    return idx == first


def _router_kernel(x_ref, wrt_ref, eb_ref, gates_ref):
    tb = x_ref.shape[0]
    wh, wl = _split_bf16(wrt_ref[...])
    xh, xl = _split_bf16(x_ref[...])
    logits = _dot_nt(wh, xh) + _dot_nt(wh, xl) + _dot_nt(wl, xh)
    scores = _sigmoid(logits)
    biased = scores + eb_ref[...]
    neg_inf = jnp.float32(-jnp.inf)

    sub = lax.broadcasted_iota(jnp.int32, (GROUP_SIZE, tb), 0)
    group_scores = []
    for g in range(N_GROUPS):
        bg = biased[g * GROUP_SIZE:(g + 1) * GROUP_SIZE, :]
        m1 = jnp.max(bg, axis=0, keepdims=True)
        rest = jnp.where(_first_argmax_hit(bg, sub, GROUP_SIZE), neg_inf, bg)
        group_scores.append(m1 + jnp.max(rest, axis=0, keepdims=True))
    gs = jnp.concatenate(group_scores, axis=0)

    g_idx = lax.broadcasted_iota(jnp.int32, (N_GROUPS, tb), 0)
    group_on = jnp.zeros((N_GROUPS, tb), jnp.bool_)
    for _ in range(TOPK_GROUPS):
        hit = _first_argmax_hit(gs, g_idx, N_GROUPS)
        group_on = jnp.logical_or(group_on, hit)
        gs = jnp.where(hit, neg_inf, gs)

    masked = jnp.concatenate(
        [jnp.where(group_on[g:g + 1, :], biased[g * GROUP_SIZE:(g + 1) * GROUP_SIZE, :], neg_inf)
         for g in range(N_GROUPS)], axis=0)
    e_idx = lax.broadcasted_iota(jnp.int32, (N_EXPERTS, tb), 0)
    chosen = jnp.zeros((N_EXPERTS, tb), jnp.bool_)
    for _ in range(TOP_K):
        hit = _first_argmax_hit(masked, e_idx, N_EXPERTS)
        chosen = jnp.logical_or(chosen, hit)
        masked = jnp.where(hit, neg_inf, masked)

    sel = jnp.where(chosen, scores, 0.0)
    gates = sel / jnp.sum(sel, axis=0, keepdims=True) * ROUTED_SCALE
    gates_ref[...] = jnp.concatenate([gates, jnp.zeros((LANES - N_EXPERTS, tb), F32)], axis=0).T


def _router(x, w_router_t, e_bias_col, *, tb):
    n = x.shape[0]
    return pl.pallas_call(
        _router_kernel,
        grid=(n // tb,),
        in_specs=[pl.BlockSpec((tb, D_MODEL), lambda i: (i, 0)),
                  pl.BlockSpec(w_router_t.shape, lambda i: (0, 0)),
                  pl.BlockSpec(e_bias_col.shape, lambda i: (0, 0))],
        out_specs=pl.BlockSpec((tb, LANES), lambda i: (i, 0)),
        out_shape=jax.ShapeDtypeStruct((n, LANES), F32),
        compiler_params=pltpu.CompilerParams(dimension_semantics=("parallel",)),
        name="router",
    )(x, w_router_t, e_bias_col)


def _swiglu_hidden(xb, wg, wu):
    hg = _dot(xb, wg)
    return hg * _sigmoid(hg) * _dot(xb, wu)


MOE_EXPERTS_PER_STEP = 4


def _moe_kernel(x_ref, gates_ref, wg_ref, wu_ref, wd_ref, sg_ref, su_ref, sd_ref, g_ref, b_ref, o_ref,
                xb_ref, acc_ref):
    step = pl.program_id(1)

    @pl.when(step == 0)
    def _():
        xb = x_ref[...].astype(BF16)
        xb_ref[...] = xb
        acc_ref[...] = _dot(_swiglu_hidden(xb, sg_ref[...], su_ref[...]).astype(BF16), sd_ref[...])

    xb = xb_ref[...]
    gates = gates_ref[...]
    group = range(MOE_EXPERTS_PER_STEP)
    hidden = [_swiglu_hidden(xb, wg_ref[j].astype(BF16), wu_ref[j].astype(BF16)) for j in group]
    total = acc_ref[...]
    for j in group:
        e = step * MOE_EXPERTS_PER_STEP + j
        gate = jnp.take_along_axis(gates, jnp.full(gates.shape, e, jnp.int32), axis=1)
        act = hidden[j] * jnp.concatenate([gate] * (EXPERT_FF // LANES), axis=1)
        total = total + _dot(act.astype(BF16), wd_ref[j].astype(BF16))
    acc_ref[...] = total

    @pl.when(step == pl.num_programs(1) - 1)
    def _():
        o_ref[...] = _layer_norm(DN_ALPHA * x_ref[...] + acc_ref[...], g_ref[...], b_ref[...])


def _moe(x, gates, layer, wg, wu, wd, sg, su, sd, ln_g, ln_b, *, tb):
    n = x.shape[0]
    group = MOE_EXPERTS_PER_STEP
    tok = lambda i, s: (i, 0)
    exp = lambda i, s: (layer, s, 0, 0)
    full = lambda a: pl.BlockSpec(a.shape, lambda i, s: (0,) * a.ndim)
    return pl.pallas_call(
        _moe_kernel,
        grid=(n // tb, N_EXPERTS // group),
        in_specs=[pl.BlockSpec((tb, D_MODEL), tok),
                  pl.BlockSpec((tb, LANES), tok),
                  pl.BlockSpec((None, group, D_MODEL, EXPERT_FF), exp),
                  pl.BlockSpec((None, group, D_MODEL, EXPERT_FF), exp),
                  pl.BlockSpec((None, group, EXPERT_FF, D_MODEL), exp),
                  full(sg), full(su), full(sd), full(ln_g), full(ln_b)],
        out_specs=pl.BlockSpec((tb, D_MODEL), tok),
        out_shape=jax.ShapeDtypeStruct((n, D_MODEL), F32),
        scratch_shapes=[pltpu.VMEM((tb, D_MODEL), BF16),
                        pltpu.VMEM((tb, D_MODEL), F32)],
        compiler_params=pltpu.CompilerParams(dimension_semantics=("parallel", "arbitrary")),
        name="moe_experts_ln",
    )(x, gates, wg, wu, wd, sg, su, sd, ln_g, ln_b)


def _rwkv_pre_kernel(x_ref, prev_ref, mu_ref, wr_ref, wk_ref, wv_ref, w1_ref, w2_ref, w0_ref,
                     a1_ref, a2_ref, a0_ref, g1_ref, g2_ref, kk_scale_ref, ka_ref, rk_ref,
                     ab_ref, av_ref, kr_ref, arb_ref, bp_ref, kpv_ref, dend_ref, g_ref, bonus_ref,
                     ext_ref, r_ref, ld_ref, k_ref, v_ref, kk_ref, b_ref, *, tb):
    i = pl.program_id(0)
    x = x_ref[...]
    ext_ref[0:SUBLANES, :] = jnp.where(i > 0, prev_ref[...], 0.0)
    ext_ref[SUBLANES:, :] = x
    ext_ref[SUBLANES:, :] = ext_ref[SUBLANES - 1:SUBLANES - 1 + tb, :] - x
    xx = ext_ref[SUBLANES:, :]
    mix = lambda j: (x + xx * mu_ref[j:j + 1, :]).astype(BF16)
    xr, xw, xk, xv, xa, xg = [mix(j) for j in range(6)]

    r = _dot(xr, wr_ref[...])
    w = w0_ref[...] + _dot(jnp.tanh(_dot(xw, w1_ref[...])).astype(BF16), w2_ref[...])
    w_log = -_softplus(-w) - 0.5
    ld_ref[...] = -jnp.exp(w_log)
    k = _dot(xk, wk_ref[...])
    v = _dot(xv, wv_ref[...])
    a = _sigmoid(a0_ref[...] + _dot(_dot(xa, a1_ref[...]).astype(BF16), a2_ref[...]))
    g_ref[...] = _dot(_sigmoid(_dot(xg, g1_ref[...])).astype(BF16), g2_ref[...]).astype(BF16)

    ones = _head_ones()
    kk = k * kk_scale_ref[...]
    kk = kk * jnp.minimum(lax.rsqrt(_head_sum(kk * kk, ones)), 1e12)
    k = k * (1.0 + (a - 1.0) * ka_ref[...])
    bonus_ref[...] = (_head_sum(r * k * rk_ref[...], ones) * v).astype(BF16)
    r_ref[...] = r
    k_ref[...] = k
    v_ref[...] = v
    kk_ref[...] = kk
    b_ref[...] = kk * a
    _scan_operands(r_ref, ld_ref, k_ref, v_ref, kk_ref, b_ref,
                   ab_ref, av_ref, kr_ref, arb_ref, bp_ref, kpv_ref, dend_ref, tb=tb)


def _rwkv_pre(x, mu, wr, wk, wv, w1, w2, w0, a1, a2, a0, g1, g2, k_k, k_a, r_k, *, tb):
    n = x.shape[0]
    row = lambda i: (i, 0)
    full = lambda a: pl.BlockSpec(a.shape, lambda i: (0,) * a.ndim)
    params = (mu, wr, wk, wv, w1, w2, w0, a1, a2, a0, g1, g2, k_k, k_a, r_k)
    per_chunk = SUBLANES / SCAN_CHUNK
    one, two, eighth = _scan_spec(1, tb), _scan_spec(2, tb), _scan_spec(per_chunk, tb)
    shape = lambda rows, dt: jax.ShapeDtypeStruct((int(n * rows), D_MODEL), dt)
    return pl.pallas_call(
        functools.partial(_rwkv_pre_kernel, tb=tb),
        grid=(n // tb,),
        in_specs=[pl.BlockSpec((tb, D_MODEL), row),
                  pl.BlockSpec((SUBLANES, D_MODEL), lambda i: (jnp.maximum(i * (tb // SUBLANES) - 1, 0), 0))]
                 + [full(p) for p in params],
        out_specs=[one, two, two, one, one, two, eighth, one, one],
        out_shape=[shape(1, F32), shape(2, F32), shape(2, BF16), shape(1, BF16), shape(1, BF16), shape(2, F32),
                   shape(per_chunk, F32), shape(1, BF16), shape(1, BF16)],
        scratch_shapes=[pltpu.VMEM((tb + SUBLANES, D_MODEL), F32)] + [pltpu.VMEM((tb, D_MODEL), F32)] * 6,
        compiler_params=pltpu.CompilerParams(dimension_semantics=("parallel",)),
        name="rwkv_projections",
    )(x, x, *params)


SCAN_CHUNK = 64
SOLVE_BLOCK = 16
N_PAIRS = D_MODEL // LANES


def _dot_tn(a, b):
    return lax.dot_general(a, b, (((0,), (0,)), ((), ())), preferred_element_type=F32)


def _split3_bf16(x):
    hi = x.astype(BF16)
    r1 = x - hi.astype(F32)
    mid = r1.astype(BF16)
    return hi, mid, (r1 - mid.astype(F32)).astype(BF16)


def _block_diag(x):
    lane = lax.broadcasted_iota(jnp.int32, x.shape, 1)
    zero = jnp.zeros_like(x)
    return jnp.concatenate([jnp.where(lane < RW_HEAD, x, zero), jnp.where(lane >= RW_HEAD, x, zero)], axis=0)


def _same_head_mask():
    r = lax.broadcasted_iota(jnp.int32, (LANES, LANES), 0)
    c = lax.broadcasted_iota(jnp.int32, (LANES, LANES), 1)
    return (r // RW_HEAD) == (c // RW_HEAD)


def _scan_operands(r_ref, ld_ref, k_ref, v_ref, kk_ref, b_ref,
                   ab_ref, av_ref, kr_ref, arb_ref, bp_ref, kpv_ref, dend_ref, *, tb):
    c = SCAN_CHUNK
    row = lax.broadcasted_iota(jnp.int32, (c, LANES), 0)
    li = lax.broadcasted_iota(jnp.int32, (c, LANES), 1) % RW_HEAD
    strict = li < row
    incl = li <= row
    r2 = lax.broadcasted_iota(jnp.int32, (c, c), 0)
    c2 = lax.broadcasted_iota(jnp.int32, (c, c), 1)
    cumsum = (c2 <= r2).astype(BF16)
    same_head = _same_head_mask()

    def chunk(ci, carry):
        rows = pl.ds(pl.multiple_of(ci * c, c), c)
        rows2 = pl.ds(pl.multiple_of(ci * 2 * c, 2 * c), 2 * c)
        rows8 = pl.ds(pl.multiple_of(ci * SUBLANES, SUBLANES), SUBLANES)
        pairs = range(N_PAIRS)
        lanes = [slice(p * LANES, (p + 1) * LANES) for p in pairs]
        ld_step = [ld_ref[rows, lanes[p]] for p in pairs]
        parts = [_split3_bf16(x) for x in ld_step]
        ld = [_dot(cumsum, hi) + _dot(cumsum, mid) + _dot(cumsum, lo) for hi, mid, lo in parts]
        ld_end = [x[c - 1:c, :] for x in ld]
        inv_d = [jnp.exp(-x) for x in ld]
        vb = [v_ref[rows, lanes[p]].astype(BF16) for p in pairs]
        kr = [jnp.concatenate([kk_ref[rows, lanes[p]] * jnp.exp(ld[p] - ld_step[p]),
                               r_ref[rows, lanes[p]] * jnp.exp(ld[p])], axis=0).astype(BF16) for p in pairs]
        a_k = [_dot_nt(kr[p], _block_diag((k_ref[rows, lanes[p]] * inv_d[p]).astype(BF16))) for p in pairs]
        a_b = [_dot_nt(kr[p], _block_diag((b_ref[rows, lanes[p]] * inv_d[p]).astype(BF16))) for p in pairs]
        for p in pairs:
            ab_ref[rows, lanes[p]] = jnp.where(strict, a_b[p][:c], 0.0)
            arb_ref[rows, lanes[p]] = jnp.where(incl, a_b[p][c:], 0.0).astype(BF16)
            kr_ref[rows2, lanes[p]] = kr[p]
        a_k = [jnp.concatenate([jnp.where(strict, x[:c], 0.0), jnp.where(incl, x[c:], 0.0)], axis=0).astype(BF16)
               for x in a_k]
        av = [_dot(a_k[p], _block_diag(vb[p])) for p in pairs]
        to_end = [jnp.exp(ld_end[p] - ld[p]) for p in pairs]
        kpv = [_dot_tn((k_ref[rows, lanes[p]] * to_end[p]).astype(BF16), vb[p]) for p in pairs]
        for p in pairs:
            av_ref[rows2, lanes[p]] = av[p]
            bp_ref[rows, lanes[p]] = (b_ref[rows, lanes[p]] * to_end[p]).astype(BF16)
            kpv_ref[rows2, lanes[p]] = jnp.where(same_head, kpv[p], 0.0)
            dend_ref[rows8, lanes[p]] = jnp.broadcast_to(jnp.exp(ld_end[p]), (SUBLANES, LANES))
        return carry

    lax.fori_loop(0, tb // c, chunk, 0)


def _scan_spec(rows_per_step, tb):
    return pl.BlockSpec((int(tb * rows_per_step), D_MODEL), lambda i: (i, 0))


def _scan_kernel(ab_ref, av_ref, kr_ref, arb_ref, bp_ref, kpv_ref, dend_ref, y_ref, st_ref, u_ref, *, tb):
    c = SCAN_CHUNK
    n_blk = c // SOLVE_BLOCK
    tiles = SOLVE_BLOCK // SUBLANES

    @pl.when(pl.program_id(0) == 0)
    def _():
        st_ref[...] = jnp.zeros_like(st_ref)

    lane8 = lax.broadcasted_iota(jnp.int32, (SUBLANES, LANES), 1)
    rr = lax.broadcasted_iota(jnp.int32, (LANES, LANES), 0)
    cc = lax.broadcasted_iota(jnp.int32, (LANES, LANES), 1)
    same_head = _same_head_mask()
    eye = rr == cc

    def chunk(ci, carry):
        row0 = pl.multiple_of(ci * c, c)
        rows = pl.ds(row0, c)
        row2 = pl.multiple_of(ci * 2 * c, 2 * c)
        rows8 = pl.ds(pl.multiple_of(ci * SUBLANES, SUBLANES), SUBLANES)
        y_state = []
        for p in range(N_PAIRS):
            lanes = slice(p * LANES, (p + 1) * LANES)
            sk = _dot(kr_ref[pl.ds(row2, 2 * c), lanes], st_ref[p].astype(BF16))
            u_ref[p] = sk[:c] + av_ref[pl.ds(row2, c), lanes]
            y_state.append(sk[c:])

        for g in range(n_blk):
            base = g * SOLVE_BLOCK
            cur = [[u_ref[p, base + s * SUBLANES:base + (s + 1) * SUBLANES, :] for s in range(tiles)]
                   for p in range(N_PAIRS)]
            for s in range(tiles):
                for j in range(SUBLANES):
                    i = base + s * SUBLANES + j
                    idx = jnp.where(lane8 >= RW_HEAD, RW_HEAD + i, i)
                    for p in range(N_PAIRS):
                        lanes = slice(p * LANES, (p + 1) * LANES)
                        solved = cur[p][s][j:j + 1, :]
                        for s2 in range(s, tiles):
                            below = ab_ref[pl.ds(row0 + base + s2 * SUBLANES, SUBLANES), lanes]
                            cur[p][s2] = cur[p][s2] - jnp.take_along_axis(below, idx, axis=1) * solved
            n_rest = c - base - SOLVE_BLOCK
            for p in range(N_PAIRS):
                lanes = slice(p * LANES, (p + 1) * LANES)
                ub = jnp.concatenate(cur[p], axis=0)
                u_ref[p, base:base + SOLVE_BLOCK, :] = ub
                if n_rest:
                    ub = ub.astype(BF16)
                    lane_b = lax.broadcasted_iota(jnp.int32, ub.shape, 1)
                    pieces = []
                    for h in range(2):
                        keep = (lane_b >= RW_HEAD) if h else (lane_b < RW_HEAD)
                        if base:
                            pieces.append(jnp.zeros((base, LANES), BF16))
                        pieces.append(jnp.where(keep, ub, jnp.zeros_like(ub)))
                        pieces.append(jnp.zeros((n_rest, LANES), BF16))
                    rest = slice(base + SOLVE_BLOCK, c)
                    lower = ab_ref[pl.ds(row0 + base + SOLVE_BLOCK, n_rest), lanes].astype(BF16)
                    u_ref[p, rest, :] = u_ref[p, rest, :] - _dot(lower, jnp.concatenate(pieces, axis=0))

        for p in range(N_PAIRS):
            lanes = slice(p * LANES, (p + 1) * LANES)
            u = u_ref[p].astype(BF16)
            y_ref[rows, lanes] = (y_state[p] + av_ref[pl.ds(row2 + c, c), lanes]
                                  - _dot(arb_ref[rows, lanes], _block_diag(u)))
            upd = kpv_ref[pl.ds(row2, 2 * c), lanes] - jnp.where(same_head, _dot_tn(bp_ref[rows, lanes], u), 0.0)
            d_end = dend_ref[rows8, lanes][0:1, :]
            d_col = jnp.sum(jnp.where(eye, d_end, 0.0), axis=1, keepdims=True)
            st_ref[p] = st_ref[p] * d_col + upd
        return carry

    lax.fori_loop(0, tb // c, chunk, 0)


def _rwkv_scan(ab, av, kr, arb, bp, kpv, dend, *, tb):
    n = ab.shape[0]
    one, two, eighth = _scan_spec(1, tb), _scan_spec(2, tb), _scan_spec(SUBLANES / SCAN_CHUNK, tb)
    return pl.pallas_call(
        functools.partial(_scan_kernel, tb=tb),
        grid=(n // tb,),
        in_specs=[one, two, two, one, one, two, eighth],
        out_specs=one,
        out_shape=jax.ShapeDtypeStruct((n, D_MODEL), F32),
        scratch_shapes=[pltpu.VMEM((N_PAIRS, LANES, LANES), F32),
                        pltpu.VMEM((N_PAIRS, SCAN_CHUNK, LANES), F32)],
        compiler_params=pltpu.CompilerParams(dimension_semantics=("arbitrary",)),
        name="rwkv_scan",
    )(ab, av, kr, arb, bp, kpv, dend)


def _rwkv_post_kernel(y_ref, g_ref, bonus_ref, x_ref, lg_ref, lb_ref, wo_ref, ng_ref, nb_ref, o_ref):
    ones = _head_ones()
    y = y_ref[...]
    inv_n = 1.0 / RW_HEAD
    c = y - _head_sum(y, ones) * inv_n
    var = _head_sum(c * c, ones) * inv_n
    yn = c * lax.rsqrt(var + GN_EPS) * lg_ref[...] + lb_ref[...]
    out = _dot(((yn + bonus_ref[...].astype(F32)) * g_ref[...].astype(F32)).astype(BF16), wo_ref[...])
    o_ref[...] = _layer_norm(DN_ALPHA * x_ref[...] + out, ng_ref[...], nb_ref[...])


def _rwkv_post(y, g, bonus, x, lnx_g, lnx_b, w_o, ln_g, ln_b, *, tb):
    n = x.shape[0]
    row = pl.BlockSpec((tb, D_MODEL), lambda i: (i, 0))
    full = lambda a: pl.BlockSpec(a.shape, lambda i: (0,) * a.ndim)
    params = (lnx_g, lnx_b, w_o, ln_g, ln_b)
    return pl.pallas_call(
        _rwkv_post_kernel,
        grid=(n // tb,),
        in_specs=[row] * 4 + [full(p) for p in params],
        out_specs=row,
        out_shape=jax.ShapeDtypeStruct((n, D_MODEL), F32),
        compiler_params=pltpu.CompilerParams(dimension_semantics=("parallel",)),
        name="rwkv_output_ln",
    )(y, g, bonus, x, *params)


def _row(a):
    return a.reshape(1, -1)


def _block_rows(n):
    rows = dict(inproj=1024, attention=2048, mix0=512, router=2048, moe=1024, rwkv_pre=256, scan=512, rwkv_post=512)
    rows = {name: min(r, n) for name, r in rows.items()}
    assert all(n % r == 0 for r in rows.values()), (n, rows)
    return rows


def _moe_layer(x, layer, rows, w_router, e_bias, w_gate, w_up, w_down, ws_gate, ws_up, ws_down, ln_g, ln_b):
    gates = _router(x, w_router[layer].T, e_bias[layer].reshape(-1, 1), tb=rows["router"])
    return _moe(x, gates, layer, w_gate, w_up, w_down,
                ws_gate[layer].astype(BF16), ws_up[layer].astype(BF16), ws_down[layer].astype(BF16),
                _row(ln_g[layer]), _row(ln_b[layer]), tb=rows["moe"])


def kernel(x, ln_mix_g, ln_mix_b, ln_ffn_g, ln_ffn_b, ab_w_in, ab_pool_w, ab_pool_scale, ab_w_out, rw_mu, rw_w_r, rw_w_k, rw_w_v, rw_w_w1, rw_w_w2, rw_w0, rw_a1, rw_a2, rw_a0, rw_g1, rw_g2, rw_k_k, rw_k_a, rw_r_k, rw_lnx_g, rw_lnx_b, rw_w_o, moe_w_router, moe_e_bias, moe_w_gate, moe_w_up, moe_w_down, moe_ws_gate, moe_ws_up, moe_ws_down):
    batch, seq, d = x.shape
    assert batch == 1 and d == D_MODEL and seq % SCAN_CHUNK == 0
    n = seq
    rows = _block_rows(n)
    h = x.reshape(n, d)
    moe_args = (moe_w_router, moe_e_bias, moe_w_gate, moe_w_up, moe_w_down, moe_ws_gate, moe_ws_up, moe_ws_down)

    u, qkv = _inproj(h, ab_w_in[0].astype(BF16), tm=rows["inproj"])
    y_sb = _stick_breaking(qkv, tq=rows["attention"])
    h = _mix0(u, y_sb, h, ab_pool_w[0].astype(BF16), _row(ab_pool_scale[0]), ab_w_out[0].astype(BF16),
              _row(ln_mix_g[0]), _row(ln_mix_b[0]), tb=rows["mix0"])
    h = _moe_layer(h, 0, rows, *moe_args, ln_ffn_g, ln_ffn_b)

    bf = lambda a: a[0].astype(BF16)
    *scan_operands, g, bonus = _rwkv_pre(
        h, rw_mu[0], bf(rw_w_r), bf(rw_w_k), bf(rw_w_v), bf(rw_w_w1), bf(rw_w_w2), _row(rw_w0[0]),
        bf(rw_a1), bf(rw_a2), _row(rw_a0[0]), bf(rw_g1), bf(rw_g2), _row(rw_k_k[0]), _row(rw_k_a[0]),
        _row(rw_r_k[0]), tb=rows["rwkv_pre"])
    y = _rwkv_scan(*scan_operands, tb=rows["scan"])
    h = _rwkv_post(y, g, bonus, h, _row(rw_lnx_g[0]), _row(rw_lnx_b[0]), bf(rw_w_o),
                   _row(ln_mix_g[1]), _row(ln_mix_b[1]), tb=rows["rwkv_post"])
    h = _moe_layer(h, 1, rows, *moe_args, ln_ffn_g, ln_ffn_b)
    return h.reshape(batch, seq, d)
```

```python
import functools

import jax
import jax.numpy as jnp
from jax import lax
from jax.experimental import pallas as pl
from jax.experimental.pallas import tpu as pltpu

F32 = jnp.float32
BF16 = jnp.bfloat16

D_MODEL = 1024
DEPTH = 2
POOL_WINDOWS = (2, 4, 8, 16)
POOL_WIDTH = 512
POOL_GROUP = 128
POOL_HALO = 16
SB_HEAD_DIM = 64
SB_WIDTH = 512
RW_HEAD = 64
GN_EPS = 64e-5
N_EXPERTS = 64
TOP_K = 8
N_GROUPS = 8
GROUP_SIZE = N_EXPERTS // N_GROUPS
TOPK_GROUPS = 4
EXPERT_FF = 256
ROUTED_SCALE = 2.5
DN_ALPHA = (2.0 * DEPTH) ** 0.25
LN_EPS = 1e-5

LANES = 128
SUBLANES = 8
EXP_ZERO_BOUND = 110.0


def _dot(a, b, **kw):
    return jnp.dot(a, b, preferred_element_type=F32, **kw)


def _dot_nt(a, b, **kw):
    return lax.dot_general(a, b, (((1,), (1,)), ((), ())), preferred_element_type=F32, **kw)


def _split_bf16(x):
    hi = x.astype(BF16)
    lo = (x - hi.astype(F32)).astype(BF16)
    return hi, lo


def _layer_norm(h, g, b):
    mu = jnp.mean(h, axis=-1, keepdims=True)
    c = h - mu
    var = jnp.mean(c * c, axis=-1, keepdims=True)
    return c * lax.rsqrt(var + LN_EPS) * g + b


def _sigmoid(x):
    return 1.0 / (1.0 + jnp.exp(-x))


def _softplus(x):
    return jnp.maximum(x, 0.0) + jnp.log(1.0 + jnp.exp(-jnp.abs(x)))


def _head_ones():
    r = lax.broadcasted_iota(jnp.int32, (LANES, LANES), 0)
    c = lax.broadcasted_iota(jnp.int32, (LANES, LANES), 1)
    return ((r // RW_HEAD) == (c // RW_HEAD)).astype(BF16)


def _head_sum(x, ones):
    hi, lo = _split_bf16(x)
    outs = []
    for j in range(x.shape[1] // LANES):
        sl = slice(j * LANES, (j + 1) * LANES)
        outs.append(_dot(hi[:, sl], ones) + _dot(lo[:, sl], ones))
    return jnp.concatenate(outs, axis=1)


def _inproj_kernel(x_ref, w_ref, u_ref, qkv_ref):
    xb = x_ref[...].astype(BF16)
    u_ref[...] = _dot(xb, w_ref[:, :POOL_WIDTH])
    qkv_ref[...] = _dot(xb, w_ref[:, POOL_WIDTH:]).astype(BF16)


def _inproj(x, w_in, *, tm):
    n = x.shape[0]
    return pl.pallas_call(
        _inproj_kernel,
        grid=(n // tm,),
        in_specs=[pl.BlockSpec((tm, D_MODEL), lambda i: (i, 0)),
                  pl.BlockSpec(w_in.shape, lambda i: (0, 0))],
        out_specs=[pl.BlockSpec((tm, POOL_WIDTH), lambda i: (i, 0)),
                   pl.BlockSpec((tm, 3 * SB_WIDTH), lambda i: (i, 0))],
        out_shape=[jax.ShapeDtypeStruct((n, POOL_WIDTH), F32),
                   jax.ShapeDtypeStruct((n, 3 * SB_WIDTH), BF16)],
        compiler_params=pltpu.CompilerParams(dimension_semantics=("parallel",)),
        name="inproj",
    )(x, w_in)


SB_TILE = 128


def _sb_kernel(q_ref, k_ref, v_ref, o_ref, carry_ref, acc_ref):
    t = SB_TILE
    n_sub = q_ref.shape[0] // t
    first = pl.program_id(1) * n_sub
    lane = lax.broadcasted_iota(jnp.int32, (t, LANES), 1)
    row = lax.broadcasted_iota(jnp.int32, (t, t), 0)
    col = lax.broadcasted_iota(jnp.int32, (t, t), 1)
    sums = jnp.concatenate([(row > col).astype(BF16), jnp.ones((t, t), BF16)], axis=1)
    scale = SB_HEAD_DIM ** -0.5
    neg_inf = jnp.float32(-jnp.inf)
    heads = (lane < SB_HEAD_DIM, lane >= SB_HEAD_DIM)

    carry_ref[...] = jnp.zeros_like(carry_ref)
    acc_ref[...] = jnp.zeros_like(acc_ref)

    def body(st):
        m, _ = st
        walks = [(s, h) for s in range(n_sub) for h in range(2)]
        j = [first + s - m for s in range(n_sub)]
        jc = [jnp.maximum(x, 0) for x in j]
        k0 = [pl.multiple_of(x * t, t) for x in jc]
        past = [col + (jc[s] - first - s) * t < row for s in range(n_sub)]
        z = [_dot_nt(jnp.where(heads[h], q_ref[s * t:(s + 1) * t, :], 0), k_ref[pl.ds(k0[s], t), :]) * scale
             for s, h in walks]
        sp = [_softplus(-x) for x in z]
        log_fail = [jnp.where(past[s], -sp[c] - z[c], 0.0) for c, (s, h) in enumerate(walks)]
        split = [_split_bf16(x) for x in log_fail]
        part = [_dot(hi, sums) + _dot(lo, sums) for hi, lo in split]
        w = [jnp.where(past[s], jnp.exp(part[c][:, :t] + carry_ref[c] - sp[c]), 0.0)
             for c, (s, h) in enumerate(walks)]
        pv = [_dot(w[c].astype(BF16), v_ref[pl.ds(k0[s], t), :]) for c, (s, h) in enumerate(walks)]
        live = jnp.full((t, t), neg_inf, F32)
        for c, (s, h) in enumerate(walks):
            acc_ref[c] += jnp.where(j[s] >= 0, pv[c], 0.0)
            carry = carry_ref[c] + part[c][:, t:]
            carry_ref[c] = carry
            live = jnp.maximum(live, jnp.where(j[s] >= 1, carry, neg_inf))
        return m + 1, jnp.max(live)

    lax.while_loop(lambda st: st[1] > -EXP_ZERO_BOUND, body, (jnp.int32(0), jnp.float32(0.0)))
    for s in range(n_sub):
        o_ref[s * t:(s + 1) * t, :] = jnp.where(heads[0], acc_ref[2 * s], acc_ref[2 * s + 1]).astype(BF16)


def _stick_breaking(qkv, *, tq):
    n = qkv.shape[0]
    n_pairs = SB_WIDTH // LANES
    n_walks = 2 * tq // SB_TILE
    return pl.pallas_call(
        _sb_kernel,
        grid=(n_pairs, n // tq),
        in_specs=[pl.BlockSpec((tq, LANES), lambda p, i: (i, p)),
                  pl.BlockSpec((n, LANES), lambda p, i: (0, n_pairs + p)),
                  pl.BlockSpec((n, LANES), lambda p, i: (0, 2 * n_pairs + p))],
        out_specs=pl.BlockSpec((tq, LANES), lambda p, i: (i, p)),
        out_shape=jax.ShapeDtypeStruct((n, SB_WIDTH), BF16),
        scratch_shapes=[pltpu.VMEM((n_walks, SB_TILE, SB_TILE), F32),
                        pltpu.VMEM((n_walks, SB_TILE, LANES), F32)],
        compiler_params=pltpu.CompilerParams(dimension_semantics=("parallel", "parallel")),
        name="stick_breaking",
    )(qkv, qkv, qkv)


def _mix0_kernel(u_ref, halo_ref, ysb_ref, x_ref, pw_ref, ps_ref, wo_ref, g_ref, b_ref, o_ref, ext_ref, *, tb):
    i = pl.program_id(0)
    u = u_ref[...]
    ext_ref[0:POOL_HALO, :] = jnp.where(i > 0, halo_ref[...], 0.0)
    ext_ref[POOL_HALO:, :] = u
    t_glob = i * tb + lax.broadcasted_iota(jnp.int32, (tb, 1), 0)
    parts = []
    for g, win in enumerate(POOL_WINDOWS):
        cols = slice(g * POOL_GROUP, (g + 1) * POOL_GROUP)
        s = u[:, cols]
        for j in range(1, win):
            s = s + ext_ref[POOL_HALO - j:POOL_HALO - j + tb, cols]
        cnt = jnp.minimum(t_glob + 1, win).astype(F32)
        diff = s / cnt - u[:, cols]
        parts.append(_dot(diff.astype(BF16), pw_ref[g]))
    y_pool = jnp.concatenate(parts, axis=1) * ps_ref[...]
    mix = _dot(y_pool.astype(BF16), wo_ref[:POOL_WIDTH, :]) + _dot(ysb_ref[...], wo_ref[POOL_WIDTH:, :])
    o_ref[...] = _layer_norm(DN_ALPHA * x_ref[...] + mix, g_ref[...], b_ref[...])


def _mix0(u, y_sb, x, pool_w, pool_scale, w_out, ln_g, ln_b, *, tb):
    n = x.shape[0]
    halo_blocks = tb // POOL_HALO
    row = lambda i: (i, 0)
    full = lambda a: pl.BlockSpec(a.shape, lambda i: (0,) * a.ndim)
    return pl.pallas_call(
        functools.partial(_mix0_kernel, tb=tb),
        grid=(n // tb,),
        in_specs=[pl.BlockSpec((tb, POOL_WIDTH), row),
                  pl.BlockSpec((POOL_HALO, POOL_WIDTH), lambda i: (jnp.maximum(i * halo_blocks - 1, 0), 0)),
                  pl.BlockSpec((tb, SB_WIDTH), row),
                  pl.BlockSpec((tb, D_MODEL), row),
                  full(pool_w), full(pool_scale), full(w_out), full(ln_g), full(ln_b)],
        out_specs=pl.BlockSpec((tb, D_MODEL), row),
        out_shape=jax.ShapeDtypeStruct((n, D_MODEL), F32),
        scratch_shapes=[pltpu.VMEM((tb + POOL_HALO, POOL_WIDTH), F32)],
        compiler_params=pltpu.CompilerParams(dimension_semantics=("parallel",)),
        name="pool_outproj_ln",
    )(u, u, y_sb, x, pool_w, pool_scale, w_out, ln_g, ln_b)


def _first_argmax_hit(v, idx, size):
    m = jnp.max(v, axis=0, keepdims=True)
    first = jnp.min(jnp.where(v == m, idx, size), axis=0, keepdims=True)
    return idx == first


def _router_kernel(x_ref, wrt_ref, eb_ref, gates_ref):
    tb = x_ref.shape[0]
    wh, wl = _split_bf16(wrt_ref[...])
    xh, xl = _split_bf16(x_ref[...])
    logits = _dot_nt(wh, xh) + _dot_nt(wh, xl) + _dot_nt(wl, xh)
    scores = _sigmoid(logits)
    biased = scores + eb_ref[...]
    neg_inf = jnp.float32(-jnp.inf)

    sub = lax.broadcasted_iota(jnp.int32, (GROUP_SIZE, tb), 0)
    group_scores = []
    for g in range(N_GROUPS):
        bg = biased[g * GROUP_SIZE:(g + 1) * GROUP_SIZE, :]
        m1 = jnp.max(bg, axis=0, keepdims=True)
        rest = jnp.where(_first_argmax_hit(bg, sub, GROUP_SIZE), neg_inf, bg)
        group_scores.append(m1 + jnp.max(rest, axis=0, keepdims=True))
    gs = jnp.concatenate(group_scores, axis=0)

    g_idx = lax.broadcasted_iota(jnp.int32, (N_GROUPS, tb), 0)
    group_on = jnp.zeros((N_GROUPS, tb), jnp.bool_)
    for _ in range(TOPK_GROUPS):
        hit = _first_argmax_hit(gs, g_idx, N_GROUPS)
        group_on = jnp.logical_or(group_on, hit)
        gs = jnp.where(hit, neg_inf, gs)

    masked = jnp.concatenate(
        [jnp.where(group_on[g:g + 1, :], biased[g * GROUP_SIZE:(g + 1) * GROUP_SIZE, :], neg_inf)
         for g in range(N_GROUPS)], axis=0)
    e_idx = lax.broadcasted_iota(jnp.int32, (N_EXPERTS, tb), 0)
    chosen = jnp.zeros((N_EXPERTS, tb), jnp.bool_)
    for _ in range(TOP_K):
        hit = _first_argmax_hit(masked, e_idx, N_EXPERTS)
        chosen = jnp.logical_or(chosen, hit)
        masked = jnp.where(hit, neg_inf, masked)

    sel = jnp.where(chosen, scores, 0.0)
    gates = sel / jnp.sum(sel, axis=0, keepdims=True) * ROUTED_SCALE
    gates_ref[...] = jnp.concatenate([gates, jnp.zeros((LANES - N_EXPERTS, tb), F32)], axis=0).T


def _router(x, w_router_t, e_bias_col, *, tb):
    n = x.shape[0]
    return pl.pallas_call(
        _router_kernel,
        grid=(n // tb,),
        in_specs=[pl.BlockSpec((tb, D_MODEL), lambda i: (i, 0)),
                  pl.BlockSpec(w_router_t.shape, lambda i: (0, 0)),
                  pl.BlockSpec(e_bias_col.shape, lambda i: (0, 0))],
        out_specs=pl.BlockSpec((tb, LANES), lambda i: (i, 0)),
        out_shape=jax.ShapeDtypeStruct((n, LANES), F32),
        compiler_params=pltpu.CompilerParams(dimension_semantics=("parallel",)),
        name="router",
    )(x, w_router_t, e_bias_col)


def _swiglu_hidden(xb, wg, wu):
    hg = _dot(xb, wg)
    return hg * _sigmoid(hg) * _dot(xb, wu)


MOE_EXPERTS_PER_STEP = 4


def _moe_kernel(x_ref, gates_ref, wg_ref, wu_ref, wd_ref, sg_ref, su_ref, sd_ref, g_ref, b_ref, o_ref,
                xb_ref, acc_ref):
    step = pl.program_id(1)

    @pl.when(step == 0)
    def _():
        xb = x_ref[...].astype(BF16)
        xb_ref[...] = xb
        acc_ref[...] = _dot(_swiglu_hidden(xb, sg_ref[...], su_ref[...]).astype(BF16), sd_ref[...])

    xb = xb_ref[...]
    gates = gates_ref[...]
    group = range(MOE_EXPERTS_PER_STEP)
    hidden = [_swiglu_hidden(xb, wg_ref[j].astype(BF16), wu_ref[j].astype(BF16)) for j in group]
    total = acc_ref[...]
    for j in group:
        e = step * MOE_EXPERTS_PER_STEP + j
        gate = jnp.take_along_axis(gates, jnp.full(gates.shape, e, jnp.int32), axis=1)
        act = hidden[j] * jnp.concatenate([gate] * (EXPERT_FF // LANES), axis=1)
        total = total + _dot(act.astype(BF16), wd_ref[j].astype(BF16))
    acc_ref[...] = total

    @pl.when(step == pl.num_programs(1) - 1)
    def _():
        o_ref[...] = _layer_norm(DN_ALPHA * x_ref[...] + acc_ref[...], g_ref[...], b_ref[...])


def _moe(x, gates, layer, wg, wu, wd, sg, su, sd, ln_g, ln_b, *, tb):
    n = x.shape[0]
    group = MOE_EXPERTS_PER_STEP
    tok = lambda i, s: (i, 0)
    exp = lambda i, s: (layer, s, 0, 0)
    full = lambda a: pl.BlockSpec(a.shape, lambda i, s: (0,) * a.ndim)
    return pl.pallas_call(
        _moe_kernel,
        grid=(n // tb, N_EXPERTS // group),
        in_specs=[pl.BlockSpec((tb, D_MODEL), tok),
                  pl.BlockSpec((tb, LANES), tok),
                  pl.BlockSpec((None, group, D_MODEL, EXPERT_FF), exp),
                  pl.BlockSpec((None, group, D_MODEL, EXPERT_FF), exp),
                  pl.BlockSpec((None, group, EXPERT_FF, D_MODEL), exp),
                  full(sg), full(su), full(sd), full(ln_g), full(ln_b)],
        out_specs=pl.BlockSpec((tb, D_MODEL), tok),
        out_shape=jax.ShapeDtypeStruct((n, D_MODEL), F32),
        scratch_shapes=[pltpu.VMEM((tb, D_MODEL), BF16),
                        pltpu.VMEM((tb, D_MODEL), F32)],
        compiler_params=pltpu.CompilerParams(dimension_semantics=("parallel", "arbitrary")),
        name="moe_experts_ln",
    )(x, gates, wg, wu, wd, sg, su, sd, ln_g, ln_b)


def _rwkv_pre_kernel(x_ref, prev_ref, mu_ref, wr_ref, wk_ref, wv_ref, w1_ref, w2_ref, w0_ref,
                     a1_ref, a2_ref, a0_ref, g1_ref, g2_ref, kk_scale_ref, ka_ref, rk_ref,
                     ab_ref, av_ref, kr_ref, arb_ref, bp_ref, kpv_ref, dend_ref, g_ref, bonus_ref,
                     ext_ref, r_ref, ld_ref, k_ref, v_ref, kk_ref, b_ref, *, tb):
    i = pl.program_id(0)
    x = x_ref[...]
    ext_ref[0:SUBLANES, :] = jnp.where(i > 0, prev_ref[...], 0.0)
    ext_ref[SUBLANES:, :] = x
    ext_ref[SUBLANES:, :] = ext_ref[SUBLANES - 1:SUBLANES - 1 + tb, :] - x
    xx = ext_ref[SUBLANES:, :]
    mix = lambda j: (x + xx * mu_ref[j:j + 1, :]).astype(BF16)
    xr, xw, xk, xv, xa, xg = [mix(j) for j in range(6)]

    r = _dot(xr, wr_ref[...])
    w = w0_ref[...] + _dot(jnp.tanh(_dot(xw, w1_ref[...])).astype(BF16), w2_ref[...])
    w_log = -_softplus(-w) - 0.5
    ld_ref[...] = -jnp.exp(w_log)
    k = _dot(xk, wk_ref[...])
    v = _dot(xv, wv_ref[...])
    a = _sigmoid(a0_ref[...] + _dot(_dot(xa, a1_ref[...]).astype(BF16), a2_ref[...]))
    g_ref[...] = _dot(_sigmoid(_dot(xg, g1_ref[...])).astype(BF16), g2_ref[...]).astype(BF16)

    ones = _head_ones()
    kk = k * kk_scale_ref[...]
    kk = kk * jnp.minimum(lax.rsqrt(_head_sum(kk * kk, ones)), 1e12)
    k = k * (1.0 + (a - 1.0) * ka_ref[...])
    bonus_ref[...] = (_head_sum(r * k * rk_ref[...], ones) * v).astype(BF16)
    r_ref[...] = r
    k_ref[...] = k
    v_ref[...] = v
    kk_ref[...] = kk
    b_ref[...] = kk * a
    _scan_operands(r_ref, ld_ref, k_ref, v_ref, kk_ref, b_ref,
                   ab_ref, av_ref, kr_ref, arb_ref, bp_ref, kpv_ref, dend_ref, tb=tb)


def _rwkv_pre(x, mu, wr, wk, wv, w1, w2, w0, a1, a2, a0, g1, g2, k_k, k_a, r_k, *, tb):
    n = x.shape[0]
    row = lambda i: (i, 0)
    full = lambda a: pl.BlockSpec(a.shape, lambda i: (0,) * a.ndim)
    params = (mu, wr, wk, wv, w1, w2, w0, a1, a2, a0, g1, g2, k_k, k_a, r_k)
    per_chunk = SUBLANES / SCAN_CHUNK
    one, two, eighth = _scan_spec(1, tb), _scan_spec(2, tb), _scan_spec(per_chunk, tb)
    shape = lambda rows, dt: jax.ShapeDtypeStruct((int(n * rows), D_MODEL), dt)
    return pl.pallas_call(
        functools.partial(_rwkv_pre_kernel, tb=tb),
        grid=(n // tb,),
        in_specs=[pl.BlockSpec((tb, D_MODEL), row),
                  pl.BlockSpec((SUBLANES, D_MODEL), lambda i: (jnp.maximum(i * (tb // SUBLANES) - 1, 0), 0))]
                 + [full(p) for p in params],
        out_specs=[one, two, two, one, one, two, eighth, one, one],
        out_shape=[shape(1, F32), shape(2, F32), shape(2, BF16), shape(1, BF16), shape(1, BF16), shape(2, F32),
                   shape(per_chunk, F32), shape(1, BF16), shape(1, BF16)],
        scratch_shapes=[pltpu.VMEM((tb + SUBLANES, D_MODEL), F32)] + [pltpu.VMEM((tb, D_MODEL), F32)] * 6,
        compiler_params=pltpu.CompilerParams(dimension_semantics=("parallel",)),
        name="rwkv_projections",
    )(x, x, *params)


SCAN_CHUNK = 64
SOLVE_BLOCK = 16
N_PAIRS = D_MODEL // LANES


def _dot_tn(a, b):
    return lax.dot_general(a, b, (((0,), (0,)), ((), ())), preferred_element_type=F32)


def _split3_bf16(x):
    hi = x.astype(BF16)
    r1 = x - hi.astype(F32)
    mid = r1.astype(BF16)
    return hi, mid, (r1 - mid.astype(F32)).astype(BF16)


def _block_diag(x):
    lane = lax.broadcasted_iota(jnp.int32, x.shape, 1)
    zero = jnp.zeros_like(x)
    return jnp.concatenate([jnp.where(lane < RW_HEAD, x, zero), jnp.where(lane >= RW_HEAD, x, zero)], axis=0)


def _same_head_mask():
    r = lax.broadcasted_iota(jnp.int32, (LANES, LANES), 0)
    c = lax.broadcasted_iota(jnp.int32, (LANES, LANES), 1)
    return (r // RW_HEAD) == (c // RW_HEAD)


def _scan_operands(r_ref, ld_ref, k_ref, v_ref, kk_ref, b_ref,
                   ab_ref, av_ref, kr_ref, arb_ref, bp_ref, kpv_ref, dend_ref, *, tb):
    c = SCAN_CHUNK
    row = lax.broadcasted_iota(jnp.int32, (c, LANES), 0)
    li = lax.broadcasted_iota(jnp.int32, (c, LANES), 1) % RW_HEAD
    strict = li < row
    incl = li <= row
    r2 = lax.broadcasted_iota(jnp.int32, (c, c), 0)
    c2 = lax.broadcasted_iota(jnp.int32, (c, c), 1)
    cumsum = (c2 <= r2).astype(BF16)
    same_head = _same_head_mask()

    n_chunks = tb // c
    together = 4 if n_chunks % 4 == 0 else 1

    def chunks(step, carry):
        units = [(step * together + j, p) for j in range(together) for p in range(N_PAIRS)]
        n_units = range(len(units))
        rows = [pl.ds(pl.multiple_of(ci * c, c), c) for ci, _ in units]
        rows2 = [pl.ds(pl.multiple_of(ci * 2 * c, 2 * c), 2 * c) for ci, _ in units]
        rows8 = [pl.ds(pl.multiple_of(ci * SUBLANES, SUBLANES), SUBLANES) for ci, _ in units]
        lanes = [slice(p * LANES, (p + 1) * LANES) for _, p in units]
        ld_step = [ld_ref[rows[i], lanes[i]] for i in n_units]
        parts = [_split3_bf16(x) for x in ld_step]
        ld = [_dot(cumsum, hi) + _dot(cumsum, mid) + _dot(cumsum, lo) for hi, mid, lo in parts]
        ld_end = [x[c - 1:c, :] for x in ld]
        inv_d = [jnp.exp(-x) for x in ld]
        vb = [v_ref[rows[i], lanes[i]].astype(BF16) for i in n_units]
        kr = [jnp.concatenate([kk_ref[rows[i], lanes[i]] * jnp.exp(ld[i] - ld_step[i]),
                               r_ref[rows[i], lanes[i]] * jnp.exp(ld[i])], axis=0).astype(BF16) for i in n_units]
        a_k = [_dot_nt(kr[i], _block_diag((k_ref[rows[i], lanes[i]] * inv_d[i]).astype(BF16))) for i in n_units]
        a_b = [_dot_nt(kr[i], _block_diag((b_ref[rows[i], lanes[i]] * inv_d[i]).astype(BF16))) for i in n_units]
        for i in n_units:
            ab_ref[rows[i], lanes[i]] = jnp.where(strict, a_b[i][:c], 0.0)
            arb_ref[rows[i], lanes[i]] = jnp.where(incl, a_b[i][c:], 0.0).astype(BF16)
            kr_ref[rows2[i], lanes[i]] = kr[i]
        a_k = [jnp.concatenate([jnp.where(strict, x[:c], 0.0), jnp.where(incl, x[c:], 0.0)], axis=0).astype(BF16)
               for x in a_k]
        av = [_dot(a_k[i], _block_diag(vb[i])) for i in n_units]
        to_end = [jnp.exp(ld_end[i] - ld[i]) for i in n_units]
        kpv = [_dot_tn((k_ref[rows[i], lanes[i]] * to_end[i]).astype(BF16), vb[i]) for i in n_units]
        for i in n_units:
            av_ref[rows2[i], lanes[i]] = av[i]
            bp_ref[rows[i], lanes[i]] = (b_ref[rows[i], lanes[i]] * to_end[i]).astype(BF16)
            kpv_ref[rows2[i], lanes[i]] = jnp.where(same_head, kpv[i], 0.0)
            dend_ref[rows8[i], lanes[i]] = jnp.broadcast_to(jnp.exp(ld_end[i]), (SUBLANES, LANES))
        return carry

    lax.fori_loop(0, n_chunks // together, chunks, 0)


def _scan_spec(rows_per_step, tb):
    return pl.BlockSpec((int(tb * rows_per_step), D_MODEL), lambda i: (i, 0))


def _scan_kernel(ab_ref, av_ref, kr_ref, arb_ref, bp_ref, kpv_ref, dend_ref, y_ref, st_ref, u_ref, *, tb):
    c = SCAN_CHUNK
    n_blk = c // SOLVE_BLOCK
    tiles = SOLVE_BLOCK // SUBLANES

    @pl.when(pl.program_id(0) == 0)
    def _():
        st_ref[...] = jnp.zeros_like(st_ref)

    lane8 = lax.broadcasted_iota(jnp.int32, (SUBLANES, LANES), 1)
    rr = lax.broadcasted_iota(jnp.int32, (LANES, LANES), 0)
    cc = lax.broadcasted_iota(jnp.int32, (LANES, LANES), 1)
    same_head = _same_head_mask()
    eye = rr == cc

    def chunk(ci, carry):
        row0 = pl.multiple_of(ci * c, c)
        rows = pl.ds(row0, c)
        row2 = pl.multiple_of(ci * 2 * c, 2 * c)
        rows8 = pl.ds(pl.multiple_of(ci * SUBLANES, SUBLANES), SUBLANES)
        y_state = []
        for p in range(N_PAIRS):
            lanes = slice(p * LANES, (p + 1) * LANES)
            sk = _dot(kr_ref[pl.ds(row2, 2 * c), lanes], st_ref[p].astype(BF16))
            u_ref[p] = sk[:c] + av_ref[pl.ds(row2, c), lanes]
            y_state.append(sk[c:])

        for g in range(n_blk):
            base = g * SOLVE_BLOCK
            cur = [[u_ref[p, base + s * SUBLANES:base + (s + 1) * SUBLANES, :] for s in range(tiles)]
                   for p in range(N_PAIRS)]
            for s in range(tiles):
                for j in range(SUBLANES):
                    i = base + s * SUBLANES + j
                    idx = jnp.where(lane8 >= RW_HEAD, RW_HEAD + i, i)
                    for p in range(N_PAIRS):
                        lanes = slice(p * LANES, (p + 1) * LANES)
                        solved = cur[p][s][j:j + 1, :]
                        for s2 in range(s, tiles):
                            below = ab_ref[pl.ds(row0 + base + s2 * SUBLANES, SUBLANES), lanes]
                            cur[p][s2] = cur[p][s2] - jnp.take_along_axis(below, idx, axis=1) * solved
            n_rest = c - base - SOLVE_BLOCK
            for p in range(N_PAIRS):
                lanes = slice(p * LANES, (p + 1) * LANES)
                ub = jnp.concatenate(cur[p], axis=0)
                u_ref[p, base:base + SOLVE_BLOCK, :] = ub
                if n_rest:
                    ub = ub.astype(BF16)
                    lane_b = lax.broadcasted_iota(jnp.int32, ub.shape, 1)
                    pieces = []
                    for h in range(2):
                        keep = (lane_b >= RW_HEAD) if h else (lane_b < RW_HEAD)
                        if base:
                            pieces.append(jnp.zeros((base, LANES), BF16))
                        pieces.append(jnp.where(keep, ub, jnp.zeros_like(ub)))
                        pieces.append(jnp.zeros((n_rest, LANES), BF16))
                    rest = slice(base + SOLVE_BLOCK, c)
                    lower = ab_ref[pl.ds(row0 + base + SOLVE_BLOCK, n_rest), lanes].astype(BF16)
                    u_ref[p, rest, :] = u_ref[p, rest, :] - _dot(lower, jnp.concatenate(pieces, axis=0))

        for p in range(N_PAIRS):
            lanes = slice(p * LANES, (p + 1) * LANES)
            u = u_ref[p].astype(BF16)
            y_ref[rows, lanes] = (y_state[p] + av_ref[pl.ds(row2 + c, c), lanes]
                                  - _dot(arb_ref[rows, lanes], _block_diag(u)))
            upd = kpv_ref[pl.ds(row2, 2 * c), lanes] - jnp.where(same_head, _dot_tn(bp_ref[rows, lanes], u), 0.0)
            d_end = dend_ref[rows8, lanes][0:1, :]
            d_col = jnp.sum(jnp.where(eye, d_end, 0.0), axis=1, keepdims=True)
            st_ref[p] = st_ref[p] * d_col + upd
        return carry

    lax.fori_loop(0, tb // c, chunk, 0)


def _rwkv_scan(ab, av, kr, arb, bp, kpv, dend, *, tb):
    n = ab.shape[0]
    one, two, eighth = _scan_spec(1, tb), _scan_spec(2, tb), _scan_spec(SUBLANES / SCAN_CHUNK, tb)
    return pl.pallas_call(
        functools.partial(_scan_kernel, tb=tb),
        grid=(n // tb,),
        in_specs=[one, two, two, one, one, two, eighth],
        out_specs=one,
        out_shape=jax.ShapeDtypeStruct((n, D_MODEL), F32),
        scratch_shapes=[pltpu.VMEM((N_PAIRS, LANES, LANES), F32),
                        pltpu.VMEM((N_PAIRS, SCAN_CHUNK, LANES), F32)],
        compiler_params=pltpu.CompilerParams(dimension_semantics=("arbitrary",)),
        name="rwkv_scan",
    )(ab, av, kr, arb, bp, kpv, dend)


def _rwkv_post_kernel(y_ref, g_ref, bonus_ref, x_ref, lg_ref, lb_ref, wo_ref, ng_ref, nb_ref, o_ref):
    ones = _head_ones()
    y = y_ref[...]
    inv_n = 1.0 / RW_HEAD
    c = y - _head_sum(y, ones) * inv_n
    var = _head_sum(c * c, ones) * inv_n
    yn = c * lax.rsqrt(var + GN_EPS) * lg_ref[...] + lb_ref[...]
    out = _dot(((yn + bonus_ref[...].astype(F32)) * g_ref[...].astype(F32)).astype(BF16), wo_ref[...])
    o_ref[...] = _layer_norm(DN_ALPHA * x_ref[...] + out, ng_ref[...], nb_ref[...])


def _rwkv_post(y, g, bonus, x, lnx_g, lnx_b, w_o, ln_g, ln_b, *, tb):
    n = x.shape[0]
    row = pl.BlockSpec((tb, D_MODEL), lambda i: (i, 0))
    full = lambda a: pl.BlockSpec(a.shape, lambda i: (0,) * a.ndim)
    params = (lnx_g, lnx_b, w_o, ln_g, ln_b)
    return pl.pallas_call(
        _rwkv_post_kernel,
        grid=(n // tb,),
        in_specs=[row] * 4 + [full(p) for p in params],
        out_specs=row,
        out_shape=jax.ShapeDtypeStruct((n, D_MODEL), F32),
        compiler_params=pltpu.CompilerParams(dimension_semantics=("parallel",)),
        name="rwkv_output_ln",
    )(y, g, bonus, x, *params)


def _row(a):
    return a.reshape(1, -1)


def _block_rows(n):
    rows = dict(inproj=1024, attention=2048, mix0=512, router=2048, moe=1024, rwkv_pre=256, scan=512, rwkv_post=512)
    rows = {name: min(r, n) for name, r in rows.items()}
    assert all(n % r == 0 for r in rows.values()), (n, rows)
    return rows


def _moe_layer(x, layer, rows, w_router, e_bias, w_gate, w_up, w_down, ws_gate, ws_up, ws_down, ln_g, ln_b):
    gates = _router(x, w_router[layer].T, e_bias[layer].reshape(-1, 1), tb=rows["router"])
    return _moe(x, gates, layer, w_gate, w_up, w_down,
                ws_gate[layer].astype(BF16), ws_up[layer].astype(BF16), ws_down[layer].astype(BF16),
                _row(ln_g[layer]), _row(ln_b[layer]), tb=rows["moe"])


def kernel(x, ln_mix_g, ln_mix_b, ln_ffn_g, ln_ffn_b, ab_w_in, ab_pool_w, ab_pool_scale, ab_w_out, rw_mu, rw_w_r, rw_w_k, rw_w_v, rw_w_w1, rw_w_w2, rw_w0, rw_a1, rw_a2, rw_a0, rw_g1, rw_g2, rw_k_k, rw_k_a, rw_r_k, rw_lnx_g, rw_lnx_b, rw_w_o, moe_w_router, moe_e_bias, moe_w_gate, moe_w_up, moe_w_down, moe_ws_gate, moe_ws_up, moe_ws_down):
    batch, seq, d = x.shape
    assert batch == 1 and d == D_MODEL and seq % SCAN_CHUNK == 0
    n = seq
    rows = _block_rows(n)
    h = x.reshape(n, d)
    moe_args = (moe_w_router, moe_e_bias, moe_w_gate, moe_w_up, moe_w_down, moe_ws_gate, moe_ws_up, moe_ws_down)

    u, qkv = _inproj(h, ab_w_in[0].astype(BF16), tm=rows["inproj"])
    y_sb = _stick_breaking(qkv, tq=rows["attention"])
    h = _mix0(u, y_sb, h, ab_pool_w[0].astype(BF16), _row(ab_pool_scale[0]), ab_w_out[0].astype(BF16),
              _row(ln_mix_g[0]), _row(ln_mix_b[0]), tb=rows["mix0"])
    h = _moe_layer(h, 0, rows, *moe_args, ln_ffn_g, ln_ffn_b)

    bf = lambda a: a[0].astype(BF16)
    *scan_operands, g, bonus = _rwkv_pre(
        h, rw_mu[0], bf(rw_w_r), bf(rw_w_k), bf(rw_w_v), bf(rw_w_w1), bf(rw_w_w2), _row(rw_w0[0]),
        bf(rw_a1), bf(rw_a2), _row(rw_a0[0]), bf(rw_g1), bf(rw_g2), _row(rw_k_k[0]), _row(rw_k_a[0]),
        _row(rw_r_k[0]), tb=rows["rwkv_pre"])
    y = _rwkv_scan(*scan_operands, tb=rows["scan"])
    h = _rwkv_post(y, g, bonus, h, _row(rw_lnx_g[0]), _row(rw_lnx_b[0]), bf(rw_w_o),
                   _row(ln_mix_g[1]), _row(ln_mix_b[1]), tb=rows["rwkv_post"])
    h = _moe_layer(h, 1, rows, *moe_args, ln_ffn_g, ln_ffn_b)
    return h.reshape(batch, seq, d)
```
